```python
import math
import jax
import jax.numpy as jnp
from jax import lax
import numpy as np

D_MODEL = 2048
BATCH = 4
SEQ = 2048
DEPTH = 2
DEC_BATCH = 8
DEC_SEQ = 4096
PAST_LEN = 128

HEAD_DIM = 64
N_HEADS_TOTAL = D_MODEL // HEAD_DIM
H_C = N_HEADS_TOTAL // 4
H_A = (N_HEADS_TOTAL - H_C) // 2
H_B = N_HEADS_TOTAL - H_C - H_A
KV_A = H_A // 4
KV_B = H_B // 4
KV_C = H_C // 4
C_PATTERNS = ((128, 1), (512, 4), (2048, 16))
N_C_GROUPS = len(C_PATTERNS)
D_MIX = (H_A + H_B + H_C) * HEAD_DIM
BLOCK = 128
WIN_A = 128
GRID_W = 64
ROPE_THETA = 10000.0
T5_BUCKETS = 32
T5_MAX_DISTANCE = 1024
N_BIAS_HEADS = H_A + N_C_GROUPS * H_C
RMS_EPS = 1e-6
NEG_INF = -1e30
SPLIT_SIZES = (
    H_A * HEAD_DIM, KV_A * HEAD_DIM, KV_A * HEAD_DIM, H_A * HEAD_DIM,
    H_B * HEAD_DIM, KV_B * HEAD_DIM, KV_B * HEAD_DIM, H_B * HEAD_DIM,
    N_C_GROUPS * H_C * HEAD_DIM, N_C_GROUPS * KV_C * HEAD_DIM, N_C_GROUPS * KV_C * HEAD_DIM, H_C * HEAD_DIM,
)
D_IN_PROJ = sum(SPLIT_SIZES)

kernel_name = "hybrid_parallel_group_encoder"


def rms_norm(x, g):
    xf = x.astype(jnp.float32)
    xf = xf * lax.rsqrt(jnp.mean(xf * xf, axis=-1, keepdims=True) + RMS_EPS)
    return xf * g.astype(jnp.float32)


def t5_bucket(rel):
    half = T5_BUCKETS // 2
    max_exact = half // 2
    n = jnp.abs(rel)
    log_ratio = jnp.log(jnp.maximum(n, 1).astype(jnp.float32) / max_exact) / math.log(T5_MAX_DISTANCE / max_exact)
    large = jnp.minimum(max_exact + (log_ratio * (half - max_exact)).astype(jnp.int32), half - 1)
    return jnp.where(rel > 0, half, 0) + jnp.where(n < max_exact, n, large)


def rotate_axis(x, ang):
    n2 = x.shape[-1] // 2
    x1, x2 = x[..., :n2], x[..., n2:]
    c = jnp.cos(ang)[None, :, None, :]
    s = jnp.sin(ang)[None, :, None, :]
    return jnp.concatenate([x1 * c - x2 * s, x1 * s + x2 * c], axis=-1)


def axial_rope(x, ang_row, ang_col):
    half = HEAD_DIM // 2
    return jnp.concatenate([rotate_axis(x[..., :half], ang_row), rotate_axis(x[..., half:], ang_col)], axis=-1)


def window_sink_attention(q, k, v, sink, rel_bias):
    B, S = q.shape[0], q.shape[1]
    nb = S // BLOCK
    R = H_A // KV_A
    span = BLOCK + 2 * WIN_A
    qb = (q * HEAD_DIM ** -0.5).reshape(B, nb, BLOCK, KV_A, R, HEAD_DIM).transpose(1, 0, 2, 3, 4, 5)
    pad = ((0, 0), (WIN_A, WIN_A), (0, 0), (0, 0))
    kp = jnp.pad(k.astype(jnp.float32), pad)
    vp = jnp.pad(v.astype(jnp.float32), pad)
    rel = jnp.arange(span, dtype=jnp.int32)[None, :] - WIN_A - jnp.arange(BLOCK, dtype=jnp.int32)[:, None]
    band = jnp.abs(rel) <= WIN_A
    bias = rel_bias[t5_bucket(rel)][:, :, :H_A]
    bias = bias.transpose(2, 0, 1).reshape(KV_A, R, BLOCK, span).astype(jnp.float32)
    sink_l = sink.astype(jnp.float32).reshape(KV_A, R, 1)

    def one_block(args):
        qblk, start = args
        kb = lax.dynamic_slice_in_dim(kp, start, span, axis=1)
        vb = lax.dynamic_slice_in_dim(vp, start, span, axis=1)
        kpos = start - WIN_A + jnp.arange(span, dtype=jnp.int32)
        mask = band & ((kpos >= 0) & (kpos < S))[None, :]
        s = jnp.einsum('bqhrd,bkhd->bhrqk', qblk, kb) + bias
        s = jnp.where(mask, s, NEG_INF)
        m = jnp.maximum(jnp.max(s, axis=-1), sink_l)
        e = jnp.exp(s - m[..., None])
        denom = jnp.sum(e, axis=-1) + jnp.exp(sink_l - m)
        return jnp.einsum('bhrqk,bkhd->bqhrd', e / denom[..., None], vb)

    starts = jnp.arange(nb, dtype=jnp.int32) * BLOCK
    o = lax.map(one_block, (qb, starts))
    return o.transpose(1, 0, 2, 3, 4, 5).reshape(B, S, H_A * HEAD_DIM)


def global_axial_attention(q, k, v):
    B, S = q.shape[0], q.shape[1]
    nb = S // BLOCK
    R = H_B // KV_B
    qb = (q * HEAD_DIM ** -0.5).reshape(B, nb, BLOCK, KV_B, R, HEAD_DIM).transpose(1, 0, 2, 3, 4, 5)
    kf = k.astype(jnp.float32)
    vf = v.astype(jnp.float32)

    def one_block(qblk):
        s = jnp.einsum('bqhrd,bkhd->bhrqk', qblk, kf)
        p = jax.nn.softmax(s, axis=-1)
        return jnp.einsum('bhrqk,bkhd->bqhrd', p, vf)

    o = lax.map(one_block, qb)
    return o.transpose(1, 0, 2, 3, 4, 5).reshape(B, S, H_B * HEAD_DIM)


def dilated_mixture_attention(q, k, v, rel_bias):
    B, S = q.shape[0], q.shape[1]
    nb = S // BLOCK
    R = H_C // KV_C
    qb = (q * HEAD_DIM ** -0.5).reshape(B, nb, BLOCK, N_C_GROUPS, KV_C, R, HEAD_DIM).transpose(1, 0, 2, 3, 4, 5, 6)
    kf = k.astype(jnp.float32).reshape(B, S, N_C_GROUPS, KV_C, HEAD_DIM)
    vf = v.astype(jnp.float32).reshape(B, S, N_C_GROUPS, KV_C, HEAD_DIM)
    k_groups = [kf[:, :, g] for g in range(N_C_GROUPS)]
    v_groups = [vf[:, :, g] for g in range(N_C_GROUPS)]
    offsets, biases = [], []
    for g, (w, d) in enumerate(C_PATTERNS):
        radius = w // (2 * d)
        off = jnp.arange(-radius, radius + 1, dtype=jnp.int32) * d
        b = rel_bias[t5_bucket(off)][:, H_A + g * H_C: H_A + (g + 1) * H_C]
        biases.append(b.T.reshape(KV_C, R, 1, off.shape[0]).astype(jnp.float32))
        offsets.append(off)

    def one_block(args):
        qblk, start = args
        qpos = start + jnp.arange(BLOCK, dtype=jnp.int32)
        outs, lses = [], []
        for g in range(N_C_GROUPS):
            idx = qpos[:, None] + offsets[g][None, :]
            valid = (idx >= 0) & (idx < S)
            idxc = jnp.clip(idx, 0, S - 1)
            kg = k_groups[g][:, idxc]
            vg = v_groups[g][:, idxc]
            s = jnp.einsum('bqhrd,bqkhd->bhrqk', qblk[:, :, g], kg) + biases[g]
            s = jnp.where(valid[None, None, None], s, NEG_INF)
            lse = jax.nn.logsumexp(s, axis=-1)
            p = jnp.exp(s - lse[..., None])
            outs.append(jnp.einsum('bhrqk,bqkhd->bqhrd', p, vg))
            lses.append(lse)
        alpha = jax.nn.softmax(jnp.stack(lses, axis=0), axis=0).transpose(0, 1, 4, 2, 3)
        return jnp.sum(jnp.stack(outs, axis=0) * alpha[..., None], axis=0)

    starts = jnp.arange(nb, dtype=jnp.int32) * BLOCK
    o = lax.map(one_block, (qb, starts))
    return o.transpose(1, 0, 2, 3, 4, 5).reshape(B, S, H_C * HEAD_DIM)


def trunk(x, ln_g, w_in, q_norm_a, k_norm_a, sink_a, q_norm_b, k_norm_b, q_norm_c, k_norm_c, rel_bias, w_out):
    B, S = x.shape[0], x.shape[1]
    rows = S // GRID_W
    row = jnp.repeat(jnp.arange(rows, dtype=jnp.int32), GRID_W)
    col = jnp.arange(S, dtype=jnp.int32) % GRID_W
    n_freq = HEAD_DIM // 4
    inv_freq = ROPE_THETA ** (-jnp.arange(n_freq, dtype=jnp.float32) / n_freq)
    ang_row = row.astype(jnp.float32)[:, None] * inv_freq[None, :]
    ang_col = col.astype(jnp.float32)[:, None] * inv_freq[None, :]
    points = np.cumsum(SPLIT_SIZES)[:-1].tolist()

    def heads(t):
        return t.reshape(B, S, -1, HEAD_DIM)

    for l in range(DEPTH):
        h = rms_norm(x, ln_g[l]).astype(x.dtype)
        proj = jnp.einsum('bsd,de->bse', h, w_in[l])
        qa, ka, va, ga, qb, kb, vb, gb, qc, kc, vc, gc = jnp.split(proj, points, axis=-1)
        ya = window_sink_attention(rms_norm(heads(qa), q_norm_a[l]), rms_norm(heads(ka), k_norm_a[l]),
                                   heads(va), sink_a[l], rel_bias)
        yb = global_axial_attention(axial_rope(rms_norm(heads(qb), q_norm_b[l]), ang_row, ang_col),
                                    axial_rope(rms_norm(heads(kb), k_norm_b[l]), ang_row, ang_col),
                                    heads(vb))
        yc = dilated_mixture_attention(rms_norm(heads(qc), q_norm_c[l]), rms_norm(heads(kc), k_norm_c[l]),
                                       heads(vc), rel_bias)
        mix = jnp.concatenate([ya * jax.nn.silu(ga.astype(jnp.float32)),
                               yb * jax.nn.silu(gb.astype(jnp.float32)),
                               yc * jax.nn.silu(gc.astype(jnp.float32))], axis=-1).astype(x.dtype)
        x = x + jnp.einsum('bse,ed->bsd', mix, w_out[l]).astype(x.dtype)
    return x


def setup_inputs(seed: int = 0) -> dict:
    key = jax.random.key(seed)
    ks = jax.random.split(key, 13)
    f32 = jnp.float32
    return {
        "x_prompt": jax.random.normal(ks[0], (BATCH, SEQ, D_MODEL), f32),
        "x_sample": jax.random.normal(ks[1], (DEC_BATCH, DEC_SEQ, D_MODEL), f32),
        "ln_g": 1.0 + 0.05 * jax.random.normal(ks[2], (DEPTH, D_MODEL), f32),
        "w_in": jax.random.normal(ks[3], (DEPTH, D_MODEL, D_IN_PROJ), f32) * D_MODEL ** -0.5,
        "q_norm_a": 1.0 + 0.05 * jax.random.normal(ks[4], (DEPTH, HEAD_DIM), f32),
        "k_norm_a": 1.0 + 0.05 * jax.random.normal(ks[5], (DEPTH, HEAD_DIM), f32),
        "sink_a": 0.5 * jax.random.normal(ks[6], (DEPTH, H_A), f32),
        "q_norm_b": 1.0 + 0.05 * jax.random.normal(ks[7], (DEPTH, HEAD_DIM), f32),
        "k_norm_b": 1.0 + 0.05 * jax.random.normal(ks[8], (DEPTH, HEAD_DIM), f32),
        "q_norm_c": 1.0 + 0.05 * jax.random.normal(ks[9], (DEPTH, HEAD_DIM), f32),
        "k_norm_c": 1.0 + 0.05 * jax.random.normal(ks[10], (DEPTH, HEAD_DIM), f32),
        "rel_bias": 0.1 * jax.random.normal(ks[11], (T5_BUCKETS, N_BIAS_HEADS), f32),
        "w_out": jax.random.normal(ks[12], (DEPTH, D_MIX, D_MODEL), f32) * D_MIX ** -0.5,
    }


def reference(x_prompt, x_sample, ln_g, w_in, q_norm_a, k_norm_a, sink_a, q_norm_b, k_norm_b,
              q_norm_c, k_norm_c, rel_bias, w_out):
    y_prompt = trunk(x_prompt, ln_g, w_in, q_norm_a, k_norm_a, sink_a, q_norm_b, k_norm_b,
                     q_norm_c, k_norm_c, rel_bias, w_out)
    y_sample = trunk(x_sample, ln_g, w_in, q_norm_a, k_norm_a, sink_a, q_norm_b, k_norm_b,
                     q_norm_c, k_norm_c, rel_bias, w_out)
    return (y_prompt, y_sample)
```

```python
import functools
import math

import numpy as np
import jax
import jax.numpy as jnp
from jax import lax
from jax.experimental import pallas as pl
from jax.experimental.pallas import tpu as pltpu

F32 = jnp.float32
BF16 = jnp.bfloat16

D_MODEL = 2048
DEPTH = 2
HEAD_DIM = 64
N_HEADS_TOTAL = D_MODEL // HEAD_DIM
H_C = N_HEADS_TOTAL // 4
H_A = (N_HEADS_TOTAL - H_C) // 2
H_B = N_HEADS_TOTAL - H_C - H_A
KV_A = H_A // 4
KV_B = H_B // 4
KV_C = H_C // 4
GQA = 4
C_PATTERNS = ((128, 1), (512, 4), (2048, 16))
N_C_GROUPS = len(C_PATTERNS)
BLOCK = 128
WIN_A = 128
GRID_W = 64
ROPE_THETA = 10000.0
T5_BUCKETS = 32
T5_MAX_DISTANCE = 1024
RMS_EPS = 1e-6
NEG_INF = -1e30
LOG2E = math.log2(math.e)
Q_SCALE = HEAD_DIM ** -0.5 * LOG2E

W_QG = GQA * HEAD_DIM
W_KV = 2 * HEAD_DIM
OFF_QA = 0
OFF_GA = OFF_QA + H_A * HEAD_DIM
OFF_QB = OFF_GA + H_A * HEAD_DIM
OFF_GB = OFF_QB + H_B * HEAD_DIM
OFF_QC = OFF_GB + H_B * HEAD_DIM
OFF_GC = OFF_QC + N_C_GROUPS * H_C * HEAD_DIM
OFF_KVA = OFF_GC + H_C * HEAD_DIM
OFF_KVB = OFF_KVA + KV_A * W_KV
OFF_KVC = OFF_KVB + KV_B * W_KV
D_IN_PROJ = OFF_KVC + N_C_GROUPS * KV_C * W_KV

VMEM_LIMIT = 48 * 1024 * 1024


def _proj_column_order():
    sizes = (H_A * 64, KV_A * 64, KV_A * 64, H_A * 64, H_B * 64, KV_B * 64, KV_B * 64, H_B * 64,
             N_C_GROUPS * H_C * 64, N_C_GROUPS * KV_C * 64, N_C_GROUPS * KV_C * 64, H_C * 64)
    o = np.concatenate([[0], np.cumsum(sizes)])
    qa, ka, va, ga, qb, kb, vb, gb, qc, kc, vc, gc = (np.arange(o[i], o[i + 1]) for i in range(12))

    def kv_pairs(k, v, n):
        return np.concatenate([np.concatenate([k[h * 64:(h + 1) * 64], v[h * 64:(h + 1) * 64]]) for h in range(n)])

    order = np.concatenate([qa, ga, qb, gb, qc, gc, kv_pairs(ka, va, KV_A), kv_pairs(kb, vb, KV_B),
                            kv_pairs(kc, vc, N_C_GROUPS * KV_C)])
    assert order.shape == (D_IN_PROJ,) and np.array_equal(np.sort(order), np.arange(D_IN_PROJ))
    return order


def _t5_bucket(rel):
    half = T5_BUCKETS // 2
    max_exact = half // 2
    n = np.abs(rel)
    log_ratio = np.log(np.maximum(n, 1).astype(np.float64) / max_exact) / math.log(T5_MAX_DISTANCE / max_exact)
    large = np.minimum(max_exact + (log_ratio * (half - max_exact)).astype(np.int32), half - 1)
    return np.where(rel > 0, half, 0) + np.where(n < max_exact, n, large)


def _band_bias(rel_bias, heads, dilation, radius):
    rel = np.arange(3 * BLOCK)[None, :] - BLOCK - np.arange(BLOCK)[:, None]
    band = np.abs(rel) <= radius
    bucket = _t5_bucket(rel * dilation)
    b = rel_bias[bucket][:, :, heads[0]:heads[0] + len(heads)].astype(F32) * LOG2E
    b = jnp.where(band[:, :, None], b, NEG_INF)
    return b.transpose(2, 0, 1).reshape(len(heads) // GQA, GQA * BLOCK, 3 * BLOCK)


def _rms(x, w):
    return x * lax.rsqrt(jnp.mean(x * x, axis=-1, keepdims=True) + RMS_EPS) * w


def _stack_heads(x):
    return jnp.concatenate([x[:, HEAD_DIM * r:HEAD_DIM * (r + 1)] for r in range(GQA)], axis=0)


def _unstack_heads(x):
    rows = x.shape[0] // GQA
    return jnp.concatenate([x[rows * r:rows * (r + 1)] for r in range(GQA)], axis=1)


def _silu(g):
    return g * jax.nn.sigmoid(g)


def _nt_dot(a, b):
    return lax.dot_general(a, b, (((1,), (1,)), ((), ())), preferred_element_type=F32)


def _inproj_kernel(x_ref, g_ref, w_ref, o_ref, h_ref):
    @pl.when(pl.program_id(1) == 0)
    def _():
        x = x_ref[...]
        h_ref[...] = _rms(x, g_ref[...]).astype(BF16)

    o_ref[...] = jnp.dot(h_ref[...], w_ref[...], preferred_element_type=F32).astype(o_ref.dtype)


def _inproj(x2, g, w, *, tm=512, tn=512):
    T = x2.shape[0]
    return pl.pallas_call(
        _inproj_kernel,
        grid=(T // tm, D_IN_PROJ // tn),
        in_specs=[pl.BlockSpec((tm, D_MODEL), lambda i, j: (i, 0)),
                  pl.BlockSpec((1, D_MODEL), lambda i, j: (0, 0)),
                  pl.BlockSpec((D_MODEL, tn), lambda i, j: (0, j))],
        out_specs=pl.BlockSpec((tm, tn), lambda i, j: (i, j)),
        out_shape=jax.ShapeDtypeStruct((T, D_IN_PROJ), BF16),
        scratch_shapes=[pltpu.VMEM((tm, D_MODEL), BF16)],
        compiler_params=pltpu.CompilerParams(dimension_semantics=("parallel", "arbitrary"),
                                             vmem_limit_bytes=VMEM_LIMIT),
        name="inproj",
    )(x2, g, w)


def _band_scores(q_ref, kl_ref, kc_ref, kr_ref, bias_ref, qw_ref, kw_ref, blk_axis):
    i = pl.program_id(blk_axis)
    nb = pl.num_programs(blk_axis)
    q = _stack_heads(q_ref[0].astype(F32))
    qn = (_rms(q, qw_ref[...]) * Q_SCALE).astype(BF16)
    kv = jnp.concatenate([kl_ref[0], kc_ref[0], kr_ref[0]], axis=0)
    kn = _rms(kv[:, :HEAD_DIM].astype(F32), kw_ref[...]).astype(BF16)
    v = kv[:, HEAD_DIM:]
    s = _nt_dot(qn, kn) + bias_ref[0]
    col = lax.broadcasted_iota(jnp.int32, s.shape, 1)
    outside = ((col < BLOCK) & (i == 0)) | ((col >= 2 * BLOCK) & (i == nb - 1))
    return jnp.where(outside, NEG_INF, s), v


def _window_sink_kernel(q_ref, kl_ref, kc_ref, kr_ref, gate_ref, bias_ref, sink_ref, qw_ref, kw_ref, o_ref):
    s, v = _band_scores(q_ref, kl_ref, kc_ref, kr_ref, bias_ref, qw_ref, kw_ref, 2)
    sink = sink_ref[0]
    m = jnp.maximum(jnp.max(s, axis=-1, keepdims=True), sink)
    e = jnp.exp2(s - m)
    denom = jnp.sum(e, axis=-1, keepdims=True) + jnp.exp2(sink - m)
    o = jnp.dot(e.astype(BF16), v, preferred_element_type=F32) / denom
    g = gate_ref[0].astype(F32)
    o_ref[0] = (_unstack_heads(o) * _silu(g)).astype(o_ref.dtype)


def _window_sink_attention(proj, bias, sink, qw, kw):
    B, S, _ = proj.shape
    nb = S // BLOCK
    q0, g0, kv0 = OFF_QA // W_QG, OFF_GA // W_QG, OFF_KVA // W_KV

    def kv_spec(shift):
        return pl.BlockSpec((1, BLOCK, W_KV),
                            lambda b, h, i: (b, jnp.clip(i + shift, 0, nb - 1), kv0 + h))

    return pl.pallas_call(
        _window_sink_kernel,
        grid=(B, KV_A, nb),
        in_specs=[pl.BlockSpec((1, BLOCK, W_QG), lambda b, h, i: (b, i, q0 + h)),
                  kv_spec(-1), kv_spec(0), kv_spec(1),
                  pl.BlockSpec((1, BLOCK, W_QG), lambda b, h, i: (b, i, g0 + h)),
                  pl.BlockSpec((1, GQA * BLOCK, 3 * BLOCK), lambda b, h, i: (h, 0, 0)),
                  pl.BlockSpec((1, GQA * BLOCK, 1), lambda b, h, i: (h, 0, 0)),
                  pl.BlockSpec((1, HEAD_DIM), lambda b, h, i: (0, 0)),
                  pl.BlockSpec((1, HEAD_DIM), lambda b, h, i: (0, 0))],
        out_specs=pl.BlockSpec((1, BLOCK, W_QG), lambda b, h, i: (b, i, h)),
        out_shape=jax.ShapeDtypeStruct((B, S, H_A * HEAD_DIM), BF16),
        compiler_params=pltpu.CompilerParams(dimension_semantics=("parallel", "parallel", "parallel"),
                                             vmem_limit_bytes=VMEM_LIMIT),
        name="mixer_a",
    )(proj, proj, proj, proj, proj, bias, sink, qw, kw)


def _dilated_kernel(q_ref, kl_ref, kc_ref, kr_ref, bias_ref, qw_ref, kw_ref, o_ref, l_ref):
    s, v = _band_scores(q_ref, kl_ref, kc_ref, kr_ref, bias_ref, qw_ref, kw_ref, 3)
    m = jnp.max(s, axis=-1, keepdims=True)
    e = jnp.exp2(s - m)
    denom = jnp.sum(e, axis=-1, keepdims=True)
    o = jnp.dot(e.astype(BF16), v, preferred_element_type=F32) / denom
    lse = m + jnp.log2(denom)
    o_ref[0] = _unstack_heads(o)
    l_ref[0] = _unstack_heads(jnp.broadcast_to(lse, o.shape))


def _dilated_group_attention(proj, bias, qw, kw, group, dilation):
    B, S, _ = proj.shape
    d = dilation
    sub = S // d
    nu = sub // BLOCK
    view = proj.reshape(B, sub, d * D_IN_PROJ)
    q0 = OFF_QC // W_QG + group * KV_C
    kv0 = OFF_KVC // W_KV + group * KV_C
    nq, nkv = D_IN_PROJ // W_QG, D_IN_PROJ // W_KV

    def kv_spec(shift):
        return pl.BlockSpec((1, BLOCK, W_KV),
                            lambda b, r, j, u: (b, jnp.clip(u + shift, 0, nu - 1), r * nkv + kv0 + j))

    out_spec = pl.BlockSpec((1, BLOCK, W_QG), lambda b, r, j, u: (b, u, r * KV_C + j))
    out_shape = jax.ShapeDtypeStruct((B, sub, d * H_C * HEAD_DIM), F32)
    o, lse = pl.pallas_call(
        _dilated_kernel,
        grid=(B, d, KV_C, nu),
        in_specs=[pl.BlockSpec((1, BLOCK, W_QG), lambda b, r, j, u: (b, u, r * nq + q0 + j)),
                  kv_spec(-1), kv_spec(0), kv_spec(1),
                  pl.BlockSpec((1, GQA * BLOCK, 3 * BLOCK), lambda b, r, j, u: (j, 0, 0)),
                  pl.BlockSpec((1, HEAD_DIM), lambda b, r, j, u: (0, 0)),
                  pl.BlockSpec((1, HEAD_DIM), lambda b, r, j, u: (0, 0))],
        out_specs=[out_spec, out_spec],
        out_shape=[out_shape, out_shape],
        compiler_params=pltpu.CompilerParams(dimension_semantics=("parallel",) * 4,
                                             vmem_limit_bytes=VMEM_LIMIT),
        name=f"mixer_c{group}",
    )(view, view, view, view, bias, qw, kw)
    return o.reshape(B, S, H_C * HEAD_DIM), lse.reshape(B, S, H_C * HEAD_DIM)


def _merge_kernel(o0_ref, o1_ref, o2_ref, l0_ref, l1_ref, l2_ref, gate_ref, out_ref):
    l0, l1, l2 = l0_ref[...], l1_ref[...], l2_ref[...]
    m = jnp.maximum(jnp.maximum(l0, l1), l2)
    w0, w1, w2 = jnp.exp2(l0 - m), jnp.exp2(l1 - m), jnp.exp2(l2 - m)
    y = (w0 * o0_ref[...] + w1 * o1_ref[...] + w2 * o2_ref[...]) / (w0 + w1 + w2)
    out_ref[...] = (y * _silu(gate_ref[...].astype(F32))).astype(out_ref.dtype)


def _merge_groups(outs, lses, proj2, *, tm=512):
    T = proj2.shape[0]
    wc = H_C * HEAD_DIM
    spec = pl.BlockSpec((tm, wc), lambda i: (i, 0))
    return pl.pallas_call(
        _merge_kernel,
        grid=(T // tm,),
        in_specs=[spec] * 6 + [pl.BlockSpec((tm, wc), lambda i: (i, OFF_GC // wc))],
        out_specs=spec,
        out_shape=jax.ShapeDtypeStruct((T, wc), BF16),
        compiler_params=pltpu.CompilerParams(dimension_semantics=("parallel",), vmem_limit_bytes=VMEM_LIMIT),
        name="mixer_c_merge",
    )(*[o.reshape(T, wc) for o in outs], *[l.reshape(T, wc) for l in lses], proj2)


def _swap16(x):
    lane = lax.broadcasted_iota(jnp.int32, x.shape, 1)
    return jnp.where((lane & 16) != 0, pltpu.roll(x, 16, 1), pltpu.roll(x, HEAD_DIM - 16, 1))


def _rope(x, cos, sin_signed):
    return x * cos + _swap16(x) * sin_signed


def _global_kernel(q_ref, kv_ref, gate_ref, cq_ref, sq_ref, ck_ref, sk_ref, qw_ref, kw_ref, o_ref,
                   k_s, v_s, *, seq, tk, prep_rows):
    @pl.when(pl.program_id(2) == 0)
    def _prepare_keys():
        def body(c, carry):
            rows = pl.ds(pl.multiple_of(c * prep_rows, prep_rows), prep_rows)
            kv = kv_ref[0, rows, :]
            kn = _rms(kv[:, :HEAD_DIM].astype(F32), kw_ref[...])
            k_s[rows, :] = _rope(kn, ck_ref[rows, :], sk_ref[rows, :]).astype(BF16)
            v_s[rows, :] = jnp.concatenate([kv[:, HEAD_DIM:], jnp.ones((prep_rows, HEAD_DIM), BF16)], axis=1)
            return carry

        lax.fori_loop(0, seq // prep_rows, body, 0)

    q = q_ref[0].astype(F32)
    cq, sq = cq_ref[...], sq_ref[...]
    heads = []
    for r in range(GQA):
        qh = _rms(q[:, HEAD_DIM * r:HEAD_DIM * (r + 1)], qw_ref[...])
        heads.append((_rope(qh, cq, sq) * Q_SCALE).astype(BF16))
    q4 = jnp.concatenate(heads, axis=0)
    rows4 = q4.shape[0]

    def chunk(c, carry):
        m, acc = carry
        rows = pl.ds(pl.multiple_of(c * tk, tk), tk)
        s = _nt_dot(q4, k_s[rows, :])
        m_new = jnp.maximum(m, jnp.max(s, axis=-1, keepdims=True))
        p = jnp.exp2(s - m_new).astype(BF16)
        acc = jnp.exp2(m - m_new) * acc + jnp.dot(p, v_s[rows, :], preferred_element_type=F32)
        return m_new, acc

    m0 = jnp.full((rows4, 1), NEG_INF, F32)
    acc0 = jnp.zeros((rows4, 2 * HEAD_DIM), F32)
    _, acc = lax.fori_loop(0, seq // tk, chunk, (m0, acc0))
    o = acc[:, :HEAD_DIM] / acc[:, HEAD_DIM:HEAD_DIM + 1]
    g = gate_ref[0].astype(F32)
    o_ref[0] = (_unstack_heads(o) * _silu(g)).astype(o_ref.dtype)


def _global_axial_attention(proj, cos, sin_signed, qw, kw, *, tq=128, tk=512):
    B, S, _ = proj.shape
    q0, g0, kv0 = OFF_QB // W_QG, OFF_GB // W_QG, OFF_KVB // W_KV
    rope_q = pl.BlockSpec((tq, HEAD_DIM), lambda b, h, i: (i, 0))
    rope_k = pl.BlockSpec((S, HEAD_DIM), lambda b, h, i: (0, 0))
    norm_w = pl.BlockSpec((1, HEAD_DIM), lambda b, h, i: (0, 0))
    return pl.pallas_call(
        functools.partial(_global_kernel, seq=S, tk=tk, prep_rows=512),
        grid=(B, KV_B, S // tq),
        in_specs=[pl.BlockSpec((1, tq, W_QG), lambda b, h, i: (b, i, q0 + h)),
                  pl.BlockSpec((1, S, W_KV), lambda b, h, i: (b, 0, kv0 + h)),
                  pl.BlockSpec((1, tq, W_QG), lambda b, h, i: (b, i, g0 + h)),
                  rope_q, rope_q, rope_k, rope_k, norm_w, norm_w],
        out_specs=pl.BlockSpec((1, tq, W_QG), lambda b, h, i: (b, i, h)),
        out_shape=jax.ShapeDtypeStruct((B, S, H_B * HEAD_DIM), BF16),
        scratch_shapes=[pltpu.VMEM((S, HEAD_DIM), BF16), pltpu.VMEM((S, 2 * HEAD_DIM), BF16)],
        compiler_params=pltpu.CompilerParams(dimension_semantics=("parallel", "parallel", "arbitrary"),
                                             vmem_limit_bytes=VMEM_LIMIT),
        name="mixer_b",
    )(proj, proj, proj, cos, sin_signed, cos, sin_signed, qw, kw)


def _outproj_kernel(x_ref, a_ref, b_ref, c_ref, wa_ref, wb_ref, wc_ref, o_ref):
    y = jnp.dot(a_ref[...], wa_ref[...], preferred_element_type=F32)
    y += jnp.dot(b_ref[...], wb_ref[...], preferred_element_type=F32)
    y += jnp.dot(c_ref[...], wc_ref[...], preferred_element_type=F32)
    o_ref[...] = x_ref[...] + y


def _outproj(x2, mix_a, mix_b, mix_c, w, *, tm=512, tn=512):
    T = x2.shape[0]
    ka, kb, kc = H_A * HEAD_DIM, H_B * HEAD_DIM, H_C * HEAD_DIM
    return pl.pallas_call(
        _outproj_kernel,
        grid=(T // tm, D_MODEL // tn),
        in_specs=[pl.BlockSpec((tm, tn), lambda i, j: (i, j)),
                  pl.BlockSpec((tm, ka), lambda i, j: (i, 0)),
                  pl.BlockSpec((tm, kb), lambda i, j: (i, 0)),
                  pl.BlockSpec((tm, kc), lambda i, j: (i, 0)),
                  pl.BlockSpec((ka, tn), lambda i, j: (0, j)),
                  pl.BlockSpec((kb, tn), lambda i, j: (1, j)),
                  pl.BlockSpec((kc, tn), lambda i, j: ((ka + kb) // kc, j))],
        out_specs=pl.BlockSpec((tm, tn), lambda i, j: (i, j)),
        out_shape=jax.ShapeDtypeStruct((T, D_MODEL), F32),
        compiler_params=pltpu.CompilerParams(dimension_semantics=("parallel", "parallel"),
                                             vmem_limit_bytes=VMEM_LIMIT),
        name="outproj",
    )(x2, mix_a, mix_b, mix_c, w, w, w)


def _rope_tables(S):
    rows = S // GRID_W
    row = jnp.repeat(jnp.arange(rows, dtype=jnp.int32), GRID_W)
    col = jnp.arange(S, dtype=jnp.int32) % GRID_W
    n_freq = HEAD_DIM // 4
    inv_freq = ROPE_THETA ** (-jnp.arange(n_freq, dtype=F32) / n_freq)
    ang_row = row.astype(F32)[:, None] * inv_freq[None, :]
    ang_col = col.astype(F32)[:, None] * inv_freq[None, :]
    cr, sr, cc, sc = jnp.cos(ang_row), jnp.sin(ang_row), jnp.cos(ang_col), jnp.sin(ang_col)
    cos = jnp.concatenate([cr, cr, cc, cc], axis=1)
    sin_signed = jnp.concatenate([-sr, sr, -sc, sc], axis=1)
    return cos, sin_signed


def _trunk(x, p):
    B, S, _ = x.shape
    T = B * S
    cos, sin_signed = _rope_tables(S)
    x2 = x.reshape(T, D_MODEL)
    for l in range(DEPTH):
        proj2 = _inproj(x2, p["ln_g"][l], p["w_in"][l])
        proj = proj2.reshape(B, S, D_IN_PROJ)
        mix_a = _window_sink_attention(proj, p["bias_a"], p["sink_a"][l], p["q_norm_a"][l], p["k_norm_a"][l])
        mix_b = _global_axial_attention(proj, cos, sin_signed, p["q_norm_b"][l], p["k_norm_b"][l])
        outs, lses = [], []
        for g, (w, d) in enumerate(C_PATTERNS):
            o, lse = _dilated_group_attention(proj, p["bias_c"][g], p["q_norm_c"][l], p["k_norm_c"][l], g, d)
            outs.append(o)
            lses.append(lse)
        mix_c = _merge_groups(outs, lses, proj2)
        x2 = _outproj(x2, mix_a.reshape(T, -1), mix_b.reshape(T, -1), mix_c, p["w_out"][l])
    return x2.reshape(B, S, D_MODEL)


def _prepare(ln_g, w_in, q_norm_a, k_norm_a, sink_a, q_norm_b, k_norm_b, q_norm_c, k_norm_c, rel_bias, w_out):
    order = _proj_column_order()
    heads_c = [list(range(H_A + g * H_C, H_A + (g + 1) * H_C)) for g in range(N_C_GROUPS)]
    p = {
        "ln_g": ln_g.reshape(DEPTH, 1, D_MODEL),
        "w_in": w_in[:, :, order].astype(BF16),
        "w_out": w_out.astype(BF16),
        "q_norm_a": q_norm_a.reshape(DEPTH, 1, HEAD_DIM), "k_norm_a": k_norm_a.reshape(DEPTH, 1, HEAD_DIM),
        "q_norm_b": q_norm_b.reshape(DEPTH, 1, HEAD_DIM), "k_norm_b": k_norm_b.reshape(DEPTH, 1, HEAD_DIM),
        "q_norm_c": q_norm_c.reshape(DEPTH, 1, HEAD_DIM), "k_norm_c": k_norm_c.reshape(DEPTH, 1, HEAD_DIM),
        "sink_a": jnp.repeat(sink_a.astype(F32) * LOG2E, BLOCK, axis=1).reshape(DEPTH, KV_A, GQA * BLOCK, 1),
        "bias_a": _band_bias(rel_bias, list(range(H_A)), 1, WIN_A),
        "bias_c": [_band_bias(rel_bias, heads_c[g], d, w // (2 * d)) for g, (w, d) in enumerate(C_PATTERNS)],
    }
    return p


def kernel(x_prompt, x_sample, ln_g, w_in, q_norm_a, k_norm_a, sink_a, q_norm_b, k_norm_b,
           q_norm_c, k_norm_c, rel_bias, w_out):
    p = _prepare(ln_g, w_in, q_norm_a, k_norm_a, sink_a, q_norm_b, k_norm_b, q_norm_c, k_norm_c, rel_bias, w_out)
    return _trunk(x_prompt, p), _trunk(x_sample, p)
```

```python
import functools
import math

import numpy as np
import jax
import jax.numpy as jnp
from jax import lax
from jax.experimental import pallas as pl
from jax.experimental.pallas import tpu as pltpu

F32 = jnp.float32
BF16 = jnp.bfloat16

D_MODEL = 2048
DEPTH = 2
HEAD_DIM = 64
N_HEADS_TOTAL = D_MODEL // HEAD_DIM
H_C = N_HEADS_TOTAL // 4
H_A = (N_HEADS_TOTAL - H_C) // 2
H_B = N_HEADS_TOTAL - H_C - H_A
KV_A = H_A // 4
KV_B = H_B // 4
KV_C = H_C // 4
GQA = 4
C_PATTERNS = ((128, 1), (512, 4), (2048, 16))
N_C_GROUPS = len(C_PATTERNS)
BLOCK = 128
LANES = 128
WIN_A = 128
GRID_W = 64
ROPE_THETA = 10000.0
T5_BUCKETS = 32
T5_MAX_DISTANCE = 1024
RMS_EPS = 1e-6
NEG_INF = -1e30
LOG2E = math.log2(math.e)
Q_SCALE = HEAD_DIM ** -0.5 * LOG2E

W_QG = GQA * HEAD_DIM
W_KV = 2 * HEAD_DIM
W_C = H_C * HEAD_DIM

OFF_QA = 0
OFF_QB = OFF_QA + H_A * HEAD_DIM
OFF_QC = OFF_QB + H_B * HEAD_DIM
OFF_KVA = OFF_QC + N_C_GROUPS * W_C
OFF_KVB = OFF_KVA + KV_A * W_KV
OFF_KVC = OFF_KVB + KV_B * W_KV
W_QKV = OFF_KVC + N_C_GROUPS * KV_C * W_KV
OFF_GA = W_QKV
OFF_GB = OFF_GA + H_A * HEAD_DIM
OFF_GC = OFF_GB + H_B * HEAD_DIM
D_IN_PROJ = OFF_GC + W_C

M_QA = 0
M_QB = M_QA + H_A * HEAD_DIM
M_QC0 = M_QB + H_B * HEAD_DIM
M_KVA = M_QC0 + W_C
M_KVB = M_KVA + KV_A * W_KV
M_KVC0 = M_KVB + KV_B * W_KV
W_MAIN = M_KVC0 + KV_C * W_KV
W_XC = W_C + KV_C * W_KV

VMEM_LIMIT = 48 * 1024 * 1024


def _proj_column_order():
    sizes = (H_A * 64, KV_A * 64, KV_A * 64, H_A * 64, H_B * 64, KV_B * 64, KV_B * 64, H_B * 64,
             N_C_GROUPS * H_C * 64, N_C_GROUPS * KV_C * 64, N_C_GROUPS * KV_C * 64, H_C * 64)
    o = np.concatenate([[0], np.cumsum(sizes)])
    qa, ka, va, ga, qb, kb, vb, gb, qc, kc, vc, gc = (np.arange(o[i], o[i + 1]) for i in range(12))

    def kv_pairs(k, v, n):
        return np.concatenate([np.concatenate([k[h * 64:(h + 1) * 64], v[h * 64:(h + 1) * 64]]) for h in range(n)])

    order = np.concatenate([qa, qb, qc, kv_pairs(ka, va, KV_A), kv_pairs(kb, vb, KV_B),
                            kv_pairs(kc, vc, N_C_GROUPS * KV_C), ga, gb, gc])
    assert order.shape == (D_IN_PROJ,) and np.array_equal(np.sort(order), np.arange(D_IN_PROJ))
    return order


def _permute_columns(w, order):
    cuts = np.flatnonzero(np.diff(order) != 1) + 1
    starts = np.concatenate([[0], cuts])
    stops = np.concatenate([cuts, [order.size]])
    return jnp.concatenate([w[..., int(order[a]):int(order[b - 1]) + 1] for a, b in zip(starts, stops)], axis=-1)


def _t5_bucket(rel):
    half = T5_BUCKETS // 2
    max_exact = half // 2
    n = np.abs(rel)
    log_ratio = np.log(np.maximum(n, 1).astype(np.float64) / max_exact) / math.log(T5_MAX_DISTANCE / max_exact)
    large = np.minimum(max_exact + (log_ratio * (half - max_exact)).astype(np.int32), half - 1)
    return np.where(rel > 0, half, 0) + np.where(n < max_exact, n, large)


def _band_bias(rel_bias, heads, dilation, radius):
    rel = np.arange(3 * BLOCK)[None, :] - BLOCK - np.arange(BLOCK)[:, None]
    band = np.abs(rel) <= radius
    bucket = _t5_bucket(rel * dilation)
    b = rel_bias[bucket][:, :, heads[0]:heads[0] + len(heads)].astype(F32) * LOG2E
    b = jnp.where(band[:, :, None], b, NEG_INF)
    return b.transpose(2, 0, 1).reshape(len(heads) // GQA, GQA * BLOCK, 3 * BLOCK)


def _rms(x, w):
    return x * lax.rsqrt(jnp.mean(x * x, axis=-1, keepdims=True) + RMS_EPS) * w


def _stack_heads(x):
    return jnp.concatenate([x[:, HEAD_DIM * r:HEAD_DIM * (r + 1)] for r in range(GQA)], axis=0)


def _unstack_heads(x):
    rows = x.shape[0] // GQA
    return jnp.concatenate([x[rows * r:rows * (r + 1)] for r in range(GQA)], axis=1)


def _silu(g):
    return g * jax.nn.sigmoid(g)


def _nt_dot(a, b):
    return lax.dot_general(a, b, (((1,), (1,)), ((), ())), preferred_element_type=F32)


def _inproj_kernel(x_ref, g_ref, w_ref, o_ref, h_ref):
    @pl.when(pl.program_id(1) == 0)
    def _():
        x = x_ref[...]
        h_ref[...] = _rms(x, g_ref[...]).astype(BF16)

    o_ref[...] = jnp.dot(h_ref[...], w_ref[...], preferred_element_type=F32).astype(o_ref.dtype)


def _inproj(x2, g, w, *, tm=512, tn=512):
    T = x2.shape[0]
    return pl.pallas_call(
        _inproj_kernel,
        grid=(T // tm, D_IN_PROJ // tn),
        in_specs=[pl.BlockSpec((tm, D_MODEL), lambda i, j: (i, 0)),
                  pl.BlockSpec((1, D_MODEL), lambda i, j: (0, 0)),
                  pl.BlockSpec((D_MODEL, tn), lambda i, j: (0, j))],
        out_specs=pl.BlockSpec((tm, tn), lambda i, j: (i, j)),
        out_shape=jax.ShapeDtypeStruct((T, D_IN_PROJ), BF16),
        scratch_shapes=[pltpu.VMEM((tm, D_MODEL), BF16)],
        compiler_params=pltpu.CompilerParams(dimension_semantics=("parallel", "arbitrary"),
                                             vmem_limit_bytes=VMEM_LIMIT),
        name="inproj",
    )(x2, g, w)


def _prep_plan():
    plan = []
    for cb in range(W_QKV // LANES):
        col = cb * LANES
        if col < OFF_QB:
            plan.append((False, False, "main", M_QA + col - OFF_QA))
        elif col < OFF_QC:
            plan.append((False, True, "main", M_QB + col - OFF_QB))
        elif col < OFF_KVA:
            g, off = divmod(col - OFF_QC, W_C)
            plan.append((False, False, "main", M_QC0 + off) if g == 0 else (False, False, f"xc{g}", off))
        elif col < OFF_KVB:
            plan.append((True, False, "main", M_KVA + col - OFF_KVA))
        elif col < OFF_KVC:
            plan.append((True, True, "main", M_KVB + col - OFF_KVB))
        else:
            g, off = divmod(col - OFF_KVC, KV_C * W_KV)
            plan.append((True, False, "main", M_KVC0 + off) if g == 0 else (True, False, f"xc{g}", W_C + off))
    return plan


def _prep_kernel(x_ref, w_ref, rope_ref, ones_ref, main_ref, xc1_ref, xc2_ref, stage_ref, *, tp, plan):
    lane = lax.broadcasted_iota(jnp.int32, (tp, LANES), 1)
    is_k = lane < HEAD_DIM
    swap_hi = (lane & 16) != 0
    seg_ones = ones_ref[...]
    n_stage = W_XC // LANES
    for cb, (is_kv, rotary, dest, col) in enumerate(plan):
        cols = slice(cb * LANES, (cb + 1) * LANES)
        x = x_ref[0, :, cols].astype(F32)
        sq = x * x
        hi = sq.astype(BF16)
        lo = (sq - hi.astype(F32)).astype(BF16)
        ssq = (jnp.dot(hi, seg_ones, preferred_element_type=F32)
               + jnp.dot(lo, seg_ones, preferred_element_type=F32))
        inv = lax.rsqrt(ssq * (1.0 / HEAD_DIM) + RMS_EPS)
        if is_kv:
            inv = jnp.where(is_k, inv, 1.0)
        y = x * inv * w_ref[:, cols]
        if rotary:
            base = 2 * LANES if is_kv else 0
            cos = rope_ref[:, base:base + LANES]
            sin_signed = rope_ref[:, base + LANES:base + 2 * LANES]
            swapped = jnp.where(swap_hi, pltpu.roll(y, 16, 1), pltpu.roll(y, LANES - 16, 1))
            y = y * cos + swapped * sin_signed
        if dest == "main":
            main_ref[0, :, col:col + LANES] = y.astype(BF16)
        else:
            stage_ref[(int(dest[2]) - 1) * n_stage + col // LANES] = y
    for g, xc_ref in ((1, xc1_ref), (2, xc2_ref)):
        d = C_PATTERNS[g][1]
        for r in range(d):
            for k in range(n_stage):
                piece = stage_ref[(g - 1) * n_stage + k, pl.ds(r, tp // d, stride=d), :]
                xc_ref[0, r, :, k * LANES:(k + 1) * LANES] = piece.astype(BF16)


def _prepare(proj, wvec, rope, seg_ones, *, tp=256):
    B, S, _ = proj.shape
    d1, d2 = C_PATTERNS[1][1], C_PATTERNS[2][1]
    plan = _prep_plan()
    return pl.pallas_call(
        functools.partial(_prep_kernel, tp=tp, plan=plan),
        grid=(B, S // tp),
        in_specs=[pl.BlockSpec((1, tp, W_QKV), lambda b, i: (b, i, 0)),
                  pl.BlockSpec((1, W_QKV), lambda b, i: (0, 0)),
                  pl.BlockSpec((tp, 4 * LANES), lambda b, i: (i, 0)),
                  pl.BlockSpec((LANES, LANES), lambda b, i: (0, 0))],
        out_specs=[pl.BlockSpec((1, tp, W_MAIN), lambda b, i: (b, i, 0)),
                   pl.BlockSpec((1, d1, tp // d1, W_XC), lambda b, i: (b, 0, i, 0)),
                   pl.BlockSpec((1, d2, tp // d2, W_XC), lambda b, i: (b, 0, i, 0))],
        out_shape=[jax.ShapeDtypeStruct((B, S, W_MAIN), BF16),
                   jax.ShapeDtypeStruct((B, d1, S // d1, W_XC), BF16),
                   jax.ShapeDtypeStruct((B, d2, S // d2, W_XC), BF16)],
        scratch_shapes=[pltpu.VMEM((2 * (W_XC // LANES), tp, LANES), F32)],
        compiler_params=pltpu.CompilerParams(dimension_semantics=("parallel", "parallel"),
                                             vmem_limit_bytes=VMEM_LIMIT),
        name="prepare",
    )(proj, wvec, rope, seg_ones)


def _band_kernel(*refs, nsub, blk_axis, with_sink):
    if with_sink:
        q_ref, kp_ref, kc_ref, kn_ref, bias_ref, sink_ref, gate_ref, o_ref = refs
    else:
        q_ref, kp_ref, kc_ref, kn_ref, bias_ref, o_ref, l_ref = refs
    i = pl.program_id(blk_axis)
    nb = pl.num_programs(blk_axis)
    kv = jnp.concatenate([kp_ref[...], kc_ref[...], kn_ref[...]], axis=0)
    bias = bias_ref[0]
    col = lax.broadcasted_iota(jnp.int32, bias.shape, 1)
    for t in range(nsub):
        rows = slice(t * BLOCK, (t + 1) * BLOCK)
        q4 = _stack_heads(q_ref[rows, :])
        win = kv[t * BLOCK:(t + 3) * BLOCK]
        s = _nt_dot(q4, win[:, :HEAD_DIM]) + bias
        if t == 0:
            s = jnp.where((col < BLOCK) & (i == 0), NEG_INF, s)
        if t == nsub - 1:
            s = jnp.where((col >= 2 * BLOCK) & (i == nb - 1), NEG_INF, s)
        m = jnp.max(s, axis=-1, keepdims=True)
        if with_sink:
            sink = sink_ref[0]
            m = jnp.maximum(jnp.broadcast_to(m, sink.shape), sink)
            e = jnp.exp2(s - jnp.concatenate([m] * 3, axis=1))
            denom = jnp.sum(e, axis=-1, keepdims=True) + jnp.exp2(sink - m)
            denom = denom[:, :HEAD_DIM]
        else:
            e = jnp.exp2(s - m)
            denom = jnp.sum(e, axis=-1, keepdims=True)
        o = jnp.dot(e.astype(BF16), win[:, HEAD_DIM:], preferred_element_type=F32) / denom
        if with_sink:
            g = gate_ref[rows, :].astype(F32)
            o_ref[rows, :] = (_unstack_heads(o) * _silu(g)).astype(o_ref.dtype)
        else:
            o_ref[rows, :] = _unstack_heads(o)
            lse = m + jnp.log2(denom)
            l_ref[rows, :] = _unstack_heads(jnp.broadcast_to(lse, o.shape))


def _band_specs(lead, lead_of, tq, nblk, q_blk, kv_blk):
    ratio = tq // BLOCK
    none = (None,) * lead

    def spec(rows, width, row_fn, col_fn):
        return pl.BlockSpec(none + (rows, width), lambda *g: lead_of(*g) + (row_fn(g[-1]), col_fn(*g)))

    return [spec(tq, W_QG, lambda u: u, q_blk),
            spec(BLOCK, W_KV, lambda u: jnp.maximum(u * ratio - 1, 0), kv_blk),
            spec(tq, W_KV, lambda u: u, kv_blk),
            spec(BLOCK, W_KV, lambda u: jnp.minimum((u + 1) * ratio, nblk * ratio - 1), kv_blk)]


def _window_sink_attention(main, proj, bias, sink):
    B, S, _ = main.shape
    tq = min(512, S)
    nblk = S // tq
    lead_of = lambda b, h, u: (b,)
    in_specs = _band_specs(1, lead_of, tq, nblk, lambda b, h, u: M_QA // W_QG + h, lambda b, h, u: M_KVA // W_KV + h)
    in_specs += [pl.BlockSpec((1, GQA * BLOCK, 3 * BLOCK), lambda b, h, u: (h, 0, 0)),
                 pl.BlockSpec((1, GQA * BLOCK, LANES), lambda b, h, u: (h, 0, 0)),
                 pl.BlockSpec((None, tq, W_QG), lambda b, h, u: (b, u, OFF_GA // W_QG + h))]
    return pl.pallas_call(
        functools.partial(_band_kernel, nsub=tq // BLOCK, blk_axis=2, with_sink=True),
        grid=(B, KV_A, nblk),
        in_specs=in_specs,
        out_specs=pl.BlockSpec((None, tq, W_QG), lambda b, h, u: (b, u, h)),
        out_shape=jax.ShapeDtypeStruct((B, S, H_A * HEAD_DIM), BF16),
        compiler_params=pltpu.CompilerParams(dimension_semantics=("parallel",) * 3, vmem_limit_bytes=VMEM_LIMIT),
        name="mixer_a",
    )(main, main, main, main, bias, sink, proj)


def _dilated_group_attention(x, bias, group):
    if group == 0:
        B, sub, _ = x.shape
        d, lead = 1, 1
        lead_of = lambda b, r, j, u: (b,)
        q0, kv0 = M_QC0 // W_QG, M_KVC0 // W_KV
        out_shape = jax.ShapeDtypeStruct((B, sub, W_C), F32)
    else:
        B, d, sub, _ = x.shape
        lead = 2
        lead_of = lambda b, r, j, u: (b, r)
        q0, kv0 = 0, W_C // W_KV
        out_shape = jax.ShapeDtypeStruct((B, d, sub, W_C), F32)
    tq = min(512, sub)
    nblk = sub // tq
    in_specs = _band_specs(lead, lead_of, tq, nblk, lambda b, r, j, u: q0 + j, lambda b, r, j, u: kv0 + j)
    in_specs += [pl.BlockSpec((1, GQA * BLOCK, 3 * BLOCK), lambda b, r, j, u: (j, 0, 0))]
    out_spec = pl.BlockSpec((None,) * lead + (tq, W_QG), lambda b, r, j, u: lead_of(b, r, j, u) + (u, j))
    return pl.pallas_call(
        functools.partial(_band_kernel, nsub=tq // BLOCK, blk_axis=3, with_sink=False),
        grid=(B, d, KV_C, nblk),
        in_specs=in_specs,
        out_specs=[out_spec, out_spec],
        out_shape=[out_shape, out_shape],
        compiler_params=pltpu.CompilerParams(dimension_semantics=("parallel",) * 4, vmem_limit_bytes=VMEM_LIMIT),
        name=f"mixer_c{group}",
    )(x, x, x, x, bias)


def _merge_kernel(o0_ref, l0_ref, o1_ref, l1_ref, o2_ref, l2_ref, gate_ref, out_ref, stage_ref, *, tm):
    nk = W_C // LANES
    for a, (ref, g) in enumerate(((o1_ref, 1), (l1_ref, 1), (o2_ref, 2), (l2_ref, 2))):
        d = C_PATTERNS[g][1]
        for r in range(d):
            for k in range(nk):
                stage_ref[a * nk + k, pl.ds(r, tm // d, stride=d), :] = ref[0, r, :, k * LANES:(k + 1) * LANES]
    for k in range(nk):
        cols = slice(k * LANES, (k + 1) * LANES)
        o0, l0 = o0_ref[0, :, cols], l0_ref[0, :, cols]
        o1, l1, o2, l2 = stage_ref[k], stage_ref[nk + k], stage_ref[2 * nk + k], stage_ref[3 * nk + k]
        m = jnp.maximum(jnp.maximum(l0, l1), l2)
        w0, w1, w2 = jnp.exp2(l0 - m), jnp.exp2(l1 - m), jnp.exp2(l2 - m)
        y = (w0 * o0 + w1 * o1 + w2 * o2) / (w0 + w1 + w2)
        out_ref[0, :, cols] = (y * _silu(gate_ref[0, :, cols].astype(F32))).astype(out_ref.dtype)


def _merge_groups(outs, lses, proj, *, tm=256):
    B, S, _ = proj.shape
    d1, d2 = C_PATTERNS[1][1], C_PATTERNS[2][1]
    tok = pl.BlockSpec((1, tm, W_C), lambda b, i: (b, i, 0))
    res1 = pl.BlockSpec((1, d1, tm // d1, W_C), lambda b, i: (b, 0, i, 0))
    res2 = pl.BlockSpec((1, d2, tm // d2, W_C), lambda b, i: (b, 0, i, 0))
    return pl.pallas_call(
        functools.partial(_merge_kernel, tm=tm),
        grid=(B, S // tm),
        in_specs=[tok, tok, res1, res1, res2, res2, pl.BlockSpec((1, tm, W_C), lambda b, i: (b, i, OFF_GC // W_C))],
        out_specs=tok,
        out_shape=jax.ShapeDtypeStruct((B, S, W_C), BF16),
        scratch_shapes=[pltpu.VMEM((4 * (W_C // LANES), tm, LANES), F32)],
        compiler_params=pltpu.CompilerParams(dimension_semantics=("parallel", "parallel"),
                                             vmem_limit_bytes=VMEM_LIMIT),
        name="mixer_c_merge",
    )(outs[0], lses[0], outs[1], lses[1], outs[2], lses[2], proj)


def _global_kernel(q_ref, kv_ref, gate_ref, o_ref, v_s, *, seq, tk, prep_rows):
    @pl.when(pl.program_id(2) == 0)
    def _stage_values():
        def body(c, carry):
            rows = pl.ds(pl.multiple_of(c * prep_rows, prep_rows), prep_rows)
            v_s[rows, :] = jnp.concatenate([kv_ref[rows, HEAD_DIM:], jnp.ones((prep_rows, HEAD_DIM), BF16)], axis=1)
            return carry

        lax.fori_loop(0, seq // prep_rows, body, 0)

    q4 = _stack_heads(q_ref[...])
    rows4 = q4.shape[0]

    def chunk(c, carry):
        m, acc = carry
        rows = pl.ds(pl.multiple_of(c * tk, tk), tk)
        s = _nt_dot(q4, kv_ref[rows, :HEAD_DIM])
        m_new = jnp.maximum(m, jnp.max(s, axis=-1, keepdims=True))
        p = jnp.exp2(s - m_new).astype(BF16)
        acc = jnp.exp2(m - m_new) * acc + jnp.dot(p, v_s[rows, :], preferred_element_type=F32)
        return m_new, acc

    m0 = jnp.full((rows4, 1), NEG_INF, F32)
    acc0 = jnp.zeros((rows4, 2 * HEAD_DIM), F32)
    _, acc = lax.fori_loop(0, seq // tk, chunk, (m0, acc0), unroll=True)
    o = acc[:, :HEAD_DIM] / acc[:, HEAD_DIM:HEAD_DIM + 1]
    g = gate_ref[...].astype(F32)
    o_ref[...] = (_unstack_heads(o) * _silu(g)).astype(o_ref.dtype)


def _global_axial_attention(main, proj, *, tq=128, tk=1024):
    B, S, _ = main.shape
    return pl.pallas_call(
        functools.partial(_global_kernel, seq=S, tk=tk, prep_rows=512),
        grid=(B, KV_B, S // tq),
        in_specs=[pl.BlockSpec((None, tq, W_QG), lambda b, h, i: (b, i, M_QB // W_QG + h)),
                  pl.BlockSpec((None, S, W_KV), lambda b, h, i: (b, 0, M_KVB // W_KV + h)),
                  pl.BlockSpec((None, tq, W_QG), lambda b, h, i: (b, i, OFF_GB // W_QG + h))],
        out_specs=pl.BlockSpec((None, tq, W_QG), lambda b, h, i: (b, i, h)),
        out_shape=jax.ShapeDtypeStruct((B, S, H_B * HEAD_DIM), BF16),
        scratch_shapes=[pltpu.VMEM((S, 2 * HEAD_DIM), BF16)],
        compiler_params=pltpu.CompilerParams(dimension_semantics=("parallel", "parallel", "arbitrary"),
                                             vmem_limit_bytes=VMEM_LIMIT),
        name="mixer_b",
    )(main, main, proj)


def _outproj_kernel(x_ref, a_ref, b_ref, c_ref, wa_ref, wb_ref, wc_ref, o_ref):
    y = jnp.dot(a_ref[...], wa_ref[...], preferred_element_type=F32)
    y += jnp.dot(b_ref[...], wb_ref[...], preferred_element_type=F32)
    y += jnp.dot(c_ref[...], wc_ref[...], preferred_element_type=F32)
    o_ref[...] = x_ref[...] + y


def _outproj(x2, mix_a, mix_b, mix_c, w, *, tm=512, tn=512):
    T = x2.shape[0]
    ka, kb, kc = H_A * HEAD_DIM, H_B * HEAD_DIM, W_C
    return pl.pallas_call(
        _outproj_kernel,
        grid=(T // tm, D_MODEL // tn),
        in_specs=[pl.BlockSpec((tm, tn), lambda i, j: (i, j)),
                  pl.BlockSpec((tm, ka), lambda i, j: (i, 0)),
                  pl.BlockSpec((tm, kb), lambda i, j: (i, 0)),
                  pl.BlockSpec((tm, kc), lambda i, j: (i, 0)),
                  pl.BlockSpec((ka, tn), lambda i, j: (0, j)),
                  pl.BlockSpec((kb, tn), lambda i, j: (1, j)),
                  pl.BlockSpec((kc, tn), lambda i, j: ((ka + kb) // kc, j))],
        out_specs=pl.BlockSpec((tm, tn), lambda i, j: (i, j)),
        out_shape=jax.ShapeDtypeStruct((T, D_MODEL), F32),
        compiler_params=pltpu.CompilerParams(dimension_semantics=("parallel", "parallel"),
                                             vmem_limit_bytes=VMEM_LIMIT),
        name="outproj",
    )(x2, mix_a, mix_b, mix_c, w, w, w)


def _rope_table(S):
    rows = S // GRID_W
    row = jnp.repeat(jnp.arange(rows, dtype=jnp.int32), GRID_W)
    col = jnp.arange(S, dtype=jnp.int32) % GRID_W
    n_freq = HEAD_DIM // 4
    inv_freq = ROPE_THETA ** (-jnp.arange(n_freq, dtype=F32) / n_freq)
    ang_row = row.astype(F32)[:, None] * inv_freq[None, :]
    ang_col = col.astype(F32)[:, None] * inv_freq[None, :]
    cr, sr, cc, sc = jnp.cos(ang_row), jnp.sin(ang_row), jnp.cos(ang_col), jnp.sin(ang_col)
    cos = jnp.concatenate([cr, cr, cc, cc], axis=1)
    sin_signed = jnp.concatenate([-sr, sr, -sc, sc], axis=1)
    one, zero = jnp.ones_like(cos), jnp.zeros_like(cos)
    return jnp.concatenate([cos, cos, sin_signed, sin_signed, cos, one, sin_signed, zero], axis=1)


def _trunk(x, p):
    B, S, _ = x.shape
    T = B * S
    rope = _rope_table(S)
    x2 = x.reshape(T, D_MODEL)
    for l in range(DEPTH):
        proj = _inproj(x2, p["ln_g"][l], p["w_in"][l]).reshape(B, S, D_IN_PROJ)
        main, xc1, xc2 = _prepare(proj, p["wvec"][l], rope, p["seg_ones"])
        mix_a = _window_sink_attention(main, proj, p["bias_a"], p["sink_a"][l])
        mix_b = _global_axial_attention(main, proj)
        outs, lses = zip(*[_dilated_group_attention(xg, p["bias_c"][g], g) for g, xg in enumerate((main, xc1, xc2))])
        mix_c = _merge_groups(outs, lses, proj)
        x2 = _outproj(x2, mix_a.reshape(T, -1), mix_b.reshape(T, -1), mix_c.reshape(T, -1), p["w_out"][l])
    return x2.reshape(B, S, D_MODEL)


def _prepare_params(ln_g, w_in, q_norm_a, k_norm_a, sink_a, q_norm_b, k_norm_b, q_norm_c, k_norm_c, rel_bias, w_out):
    heads_c = [list(range(H_A + g * H_C, H_A + (g + 1) * H_C)) for g in range(N_C_GROUPS)]
    one = jnp.ones((DEPTH, HEAD_DIM), F32)

    def q_cols(w, n):
        return jnp.tile(w.astype(F32) * Q_SCALE, (1, n))

    def kv_cols(w, n):
        return jnp.tile(jnp.concatenate([w.astype(F32), one], axis=1), (1, n))

    wvec = jnp.concatenate([q_cols(q_norm_a, H_A), q_cols(q_norm_b, H_B), q_cols(q_norm_c, N_C_GROUPS * H_C),
                            kv_cols(k_norm_a, KV_A), kv_cols(k_norm_b, KV_B), kv_cols(k_norm_c, N_C_GROUPS * KV_C)],
                           axis=1)
    seg = np.arange(LANES) // HEAD_DIM
    return {
        "ln_g": ln_g.reshape(DEPTH, 1, D_MODEL),
        "w_in": _permute_columns(w_in, _proj_column_order()).astype(BF16),
        "w_out": w_out.astype(BF16),
        "wvec": wvec.reshape(DEPTH, 1, W_QKV),
        "seg_ones": jnp.asarray(seg[:, None] == seg[None, :], BF16),
        "sink_a": jnp.broadcast_to(jnp.repeat(sink_a.astype(F32) * LOG2E, BLOCK, axis=1)[:, :, None],
                                   (DEPTH, H_A * BLOCK, LANES)).reshape(DEPTH, KV_A, GQA * BLOCK, LANES),
        "bias_a": _band_bias(rel_bias, list(range(H_A)), 1, WIN_A),
        "bias_c": [_band_bias(rel_bias, heads_c[g], d, w // (2 * d)) for g, (w, d) in enumerate(C_PATTERNS)],
    }


def kernel(x_prompt, x_sample, ln_g, w_in, q_norm_a, k_norm_a, sink_a, q_norm_b, k_norm_b,
           q_norm_c, k_norm_c, rel_bias, w_out):
    p = _prepare_params(ln_g, w_in, q_norm_a, k_norm_a, sink_a, q_norm_b, k_norm_b, q_norm_c, k_norm_c,
                        rel_bias, w_out)
    return _trunk(x_prompt, p), _trunk(x_sample, p)
```

```python
import functools
import math

import numpy as np
import jax
import jax.numpy as jnp
from jax import lax
from jax.experimental import pallas as pl
from jax.experimental.pallas import tpu as pltpu

F32 = jnp.float32
BF16 = jnp.bfloat16

D_MODEL = 2048
DEPTH = 2
HEAD_DIM = 64
N_HEADS_TOTAL = D_MODEL // HEAD_DIM
H_C = N_HEADS_TOTAL // 4
H_A = (N_HEADS_TOTAL - H_C) // 2
H_B = N_HEADS_TOTAL - H_C - H_A
KV_A = H_A // 4
KV_B = H_B // 4
KV_C = H_C // 4
GQA = 4
C_PATTERNS = ((128, 1), (512, 4), (2048, 16))
N_C_GROUPS = len(C_PATTERNS)
C_RADIUS = 64
assert all(w // (2 * d) == C_RADIUS for w, d in C_PATTERNS)
BLOCK = 128
LANES = 128
WIN_A = 128
GRID_W = 64
ROPE_THETA = 10000.0
T5_BUCKETS = 32
T5_MAX_DISTANCE = 1024
RMS_EPS = 1e-6
NEG_INF = -1e30
LOG2E = math.log2(math.e)
Q_SCALE = HEAD_DIM ** -0.5 * LOG2E

W_QG = GQA * HEAD_DIM
W_KV = 2 * HEAD_DIM
W_C = H_C * HEAD_DIM

OFF_QA = 0
OFF_QB = OFF_QA + H_A * HEAD_DIM
OFF_QC = OFF_QB + H_B * HEAD_DIM
OFF_KVA = OFF_QC + N_C_GROUPS * W_C
OFF_KVB = OFF_KVA + KV_A * W_KV
OFF_KVC = OFF_KVB + KV_B * W_KV
W_QKV = OFF_KVC + N_C_GROUPS * KV_C * W_KV
OFF_GA = W_QKV
OFF_GB = OFF_GA + H_A * HEAD_DIM
OFF_GC = OFF_GB + H_B * HEAD_DIM
D_IN_PROJ = OFF_GC + W_C

M_QA = 0
M_QB = M_QA + H_A * HEAD_DIM
M_QC0 = M_QB + H_B * HEAD_DIM
M_KVA = M_QC0 + W_C
M_KVB = M_KVA + KV_A * W_KV
M_KVC0 = M_KVB + KV_B * W_KV
W_MAIN = M_KVC0 + KV_C * W_KV
W_XC = W_C + KV_C * W_KV

VMEM_LIMIT = 48 * 1024 * 1024


def _proj_column_order():
    sizes = (H_A * 64, KV_A * 64, KV_A * 64, H_A * 64, H_B * 64, KV_B * 64, KV_B * 64, H_B * 64,
             N_C_GROUPS * H_C * 64, N_C_GROUPS * KV_C * 64, N_C_GROUPS * KV_C * 64, H_C * 64)
    o = np.concatenate([[0], np.cumsum(sizes)])
    qa, ka, va, ga, qb, kb, vb, gb, qc, kc, vc, gc = (np.arange(o[i], o[i + 1]) for i in range(12))

    def kv_pairs(k, v, n):
        return np.concatenate([np.concatenate([k[h * 64:(h + 1) * 64], v[h * 64:(h + 1) * 64]]) for h in range(n)])

    order = np.concatenate([qa, qb, qc, kv_pairs(ka, va, KV_A), kv_pairs(kb, vb, KV_B),
                            kv_pairs(kc, vc, N_C_GROUPS * KV_C), ga, gb, gc])
    assert order.shape == (D_IN_PROJ,) and np.array_equal(np.sort(order), np.arange(D_IN_PROJ))
    return order


def _permute_columns(w, order):
    cuts = np.flatnonzero(np.diff(order) != 1) + 1
    starts = np.concatenate([[0], cuts])
    stops = np.concatenate([cuts, [order.size]])
    return jnp.concatenate([w[..., int(order[a]):int(order[b - 1]) + 1] for a, b in zip(starts, stops)], axis=-1)


def _t5_bucket(rel):
    half = T5_BUCKETS // 2
    max_exact = half // 2
    n = np.abs(rel)
    log_ratio = np.log(np.maximum(n, 1).astype(np.float64) / max_exact) / math.log(T5_MAX_DISTANCE / max_exact)
    large = np.minimum(max_exact + (log_ratio * (half - max_exact)).astype(np.int32), half - 1)
    return np.where(rel > 0, half, 0) + np.where(n < max_exact, n, large)


def _band_bias(rel_bias, heads, dilation, radius, halo):
    width = BLOCK + 2 * halo
    n = width + BLOCK - 1
    rel = np.arange(n) - (halo + BLOCK - 1)
    tab = rel_bias[_t5_bucket(rel * dilation)][:, heads[0]:heads[0] + len(heads)].astype(F32) * LOG2E
    tab = jnp.where((np.abs(rel) <= radius)[:, None], tab, NEG_INF)
    stream = jnp.tile(jnp.pad(tab, ((0, 1), (0, 0))), (BLOCK, 1))[:BLOCK * n]
    b = stream.reshape(BLOCK, n, len(heads))[:, BLOCK - 1:, :]
    return b.transpose(2, 0, 1).reshape(len(heads) // GQA, GQA * BLOCK, width)


def _rms(x, w):
    return x * lax.rsqrt(jnp.mean(x * x, axis=-1, keepdims=True) + RMS_EPS) * w


def _stack_heads(x):
    return jnp.concatenate([x[:, HEAD_DIM * r:HEAD_DIM * (r + 1)] for r in range(GQA)], axis=0)


def _unstack_heads(x):
    rows = x.shape[0] // GQA
    return jnp.concatenate([x[rows * r:rows * (r + 1)] for r in range(GQA)], axis=1)


def _silu(g):
    return g * jax.nn.sigmoid(g)


def _nt_dot(a, b):
    return lax.dot_general(a, b, (((1,), (1,)), ((), ())), preferred_element_type=F32)


def _inproj_kernel(x_ref, g_ref, w_ref, o_ref, h_ref):
    @pl.when(pl.program_id(1) == 0)
    def _():
        x = x_ref[...]
        h_ref[...] = _rms(x, g_ref[...]).astype(BF16)

    o_ref[...] = jnp.dot(h_ref[...], w_ref[...], preferred_element_type=F32).astype(o_ref.dtype)


def _inproj(x2, g, w, *, tm=1024, tn=512):
    T = x2.shape[0]
    return pl.pallas_call(
        _inproj_kernel,
        grid=(T // tm, D_IN_PROJ // tn),
        in_specs=[pl.BlockSpec((tm, D_MODEL), lambda i, j: (i, 0)),
                  pl.BlockSpec((1, D_MODEL), lambda i, j: (0, 0)),
                  pl.BlockSpec((D_MODEL, tn), lambda i, j: (0, j))],
        out_specs=pl.BlockSpec((tm, tn), lambda i, j: (i, j)),
        out_shape=jax.ShapeDtypeStruct((T, D_IN_PROJ), BF16),
        scratch_shapes=[pltpu.VMEM((tm, D_MODEL), BF16)],
        compiler_params=pltpu.CompilerParams(dimension_semantics=("parallel", "arbitrary"),
                                             vmem_limit_bytes=VMEM_LIMIT),
        name="inproj",
    )(x2, g, w)


def _prep_plan():
    plan = []
    for cb in range(W_QKV // LANES):
        col = cb * LANES
        if col < OFF_QB:
            plan.append((False, False, "main", M_QA + col - OFF_QA))
        elif col < OFF_QC:
            plan.append((False, True, "main", M_QB + col - OFF_QB))
        elif col < OFF_KVA:
            g, off = divmod(col - OFF_QC, W_C)
            plan.append((False, False, "main", M_QC0 + off) if g == 0 else (False, False, f"xc{g}", off))
        elif col < OFF_KVB:
            plan.append((True, False, "main", M_KVA + col - OFF_KVA))
        elif col < OFF_KVC:
            plan.append((True, True, "main", M_KVB + col - OFF_KVB))
        else:
            g, off = divmod(col - OFF_KVC, KV_C * W_KV)
            plan.append((True, False, "main", M_KVC0 + off) if g == 0 else (True, False, f"xc{g}", W_C + off))
    return plan


def _prep_kernel(x_ref, w_ref, rope_ref, ones_ref, main_ref, xc1_ref, xc2_ref, stage_ref, *, tp, plan):
    lane = lax.broadcasted_iota(jnp.int32, (tp, LANES), 1)
    is_k = lane < HEAD_DIM
    swap_hi = (lane & 16) != 0
    seg_ones = ones_ref[...]
    n_stage = W_XC // LANES
    for cb, (is_kv, rotary, dest, col) in enumerate(plan):
        cols = slice(cb * LANES, (cb + 1) * LANES)
        x = x_ref[0, :, cols].astype(F32)
        sq = x * x
        hi = sq.astype(BF16)
        lo = (sq - hi.astype(F32)).astype(BF16)
        ssq = (jnp.dot(hi, seg_ones, preferred_element_type=F32)
               + jnp.dot(lo, seg_ones, preferred_element_type=F32))
        inv = lax.rsqrt(ssq * (1.0 / HEAD_DIM) + RMS_EPS)
        if is_kv:
            inv = jnp.where(is_k, inv, 1.0)
        y = x * inv * w_ref[:, cols]
        if rotary:
            base = 2 * LANES if is_kv else 0
            cos = rope_ref[:, base:base + LANES]
            sin_signed = rope_ref[:, base + LANES:base + 2 * LANES]
            swapped = jnp.where(swap_hi, pltpu.roll(y, 16, 1), pltpu.roll(y, LANES - 16, 1))
            y = y * cos + swapped * sin_signed
        if dest == "main":
            main_ref[0, :, col:col + LANES] = y.astype(BF16)
        else:
            stage_ref[(int(dest[2]) - 1) * n_stage + col // LANES] = y
    for g, xc_ref in ((1, xc1_ref), (2, xc2_ref)):
        d = C_PATTERNS[g][1]
        for r in range(d):
            for k in range(n_stage):
                piece = stage_ref[(g - 1) * n_stage + k, pl.ds(r, tp // d, stride=d), :]
                xc_ref[0, r, :, k * LANES:(k + 1) * LANES] = piece.astype(BF16)


def _prepare(proj, wvec, rope, seg_ones, *, tp=256):
    B, S, _ = proj.shape
    d1, d2 = C_PATTERNS[1][1], C_PATTERNS[2][1]
    plan = _prep_plan()
    return pl.pallas_call(
        functools.partial(_prep_kernel, tp=tp, plan=plan),
        grid=(B, S // tp),
        in_specs=[pl.BlockSpec((1, tp, W_QKV), lambda b, i: (b, i, 0)),
                  pl.BlockSpec((1, W_QKV), lambda b, i: (0, 0)),
                  pl.BlockSpec((tp, 4 * LANES), lambda b, i: (i, 0)),
                  pl.BlockSpec((LANES, LANES), lambda b, i: (0, 0))],
        out_specs=[pl.BlockSpec((1, tp, W_MAIN), lambda b, i: (b, i, 0)),
                   pl.BlockSpec((1, d1, tp // d1, W_XC), lambda b, i: (b, 0, i, 0)),
                   pl.BlockSpec((1, d2, tp // d2, W_XC), lambda b, i: (b, 0, i, 0))],
        out_shape=[jax.ShapeDtypeStruct((B, S, W_MAIN), BF16),
                   jax.ShapeDtypeStruct((B, d1, S // d1, W_XC), BF16),
                   jax.ShapeDtypeStruct((B, d2, S // d2, W_XC), BF16)],
        scratch_shapes=[pltpu.VMEM((2 * (W_XC // LANES), tp, LANES), F32)],
        compiler_params=pltpu.CompilerParams(dimension_semantics=("parallel", "parallel"),
                                             vmem_limit_bytes=VMEM_LIMIT),
        name="prepare",
    )(proj, wvec, rope, seg_ones)


def _band_kernel(*refs, nsub, blk_axis, with_sink, halo):
    if with_sink:
        q_ref, kp_ref, kc_ref, kn_ref, bias_ref, sink_ref, gate_ref, o_ref = refs
    else:
        q_ref, kp_ref, kc_ref, kn_ref, bias_ref, o_ref, l_ref = refs
    i = pl.program_id(blk_axis)
    nb = pl.num_programs(blk_axis)
    kv = jnp.concatenate([kp_ref[...], kc_ref[...], kn_ref[...]], axis=0)
    bias = bias_ref[0]
    width = BLOCK + 2 * halo
    col = lax.broadcasted_iota(jnp.int32, bias.shape, 1)
    for t in range(nsub):
        rows = slice(t * BLOCK, (t + 1) * BLOCK)
        q4 = _stack_heads(q_ref[rows, :])
        start = (t + 1) * BLOCK - halo
        win = kv[start:start + width]
        s = _nt_dot(q4, win[:, :HEAD_DIM]) + bias
        if t == 0:
            s = jnp.where((col < halo) & (i == 0), NEG_INF, s)
        if t == nsub - 1:
            s = jnp.where((col >= BLOCK + halo) & (i == nb - 1), NEG_INF, s)
        m = jnp.max(s, axis=-1, keepdims=True)
        if with_sink:
            sink = sink_ref[0]
            m = jnp.maximum(jnp.broadcast_to(m, sink.shape), sink)
            e = jnp.exp2(s - jnp.concatenate([m] * (width // LANES), axis=1))
            denom = jnp.sum(e, axis=-1, keepdims=True) + jnp.exp2(sink - m)
            denom = denom[:, :HEAD_DIM]
        else:
            e = jnp.exp2(s - m)
            denom = jnp.sum(e, axis=-1, keepdims=True)
        o = jnp.dot(e.astype(BF16), win[:, HEAD_DIM:], preferred_element_type=F32) / denom
        if with_sink:
            g = gate_ref[rows, :].astype(F32)
            o_ref[rows, :] = (_unstack_heads(o) * _silu(g)).astype(o_ref.dtype)
        else:
            o_ref[rows, :] = _unstack_heads(o)
            lse = m + jnp.log2(denom)
            l_ref[rows, :] = _unstack_heads(jnp.broadcast_to(lse, o.shape))


def _band_specs(lead, lead_of, tq, nblk, q_blk, kv_blk):
    ratio = tq // BLOCK
    none = (None,) * lead

    def spec(rows, width, row_fn, col_fn):
        return pl.BlockSpec(none + (rows, width), lambda *g: lead_of(*g) + (row_fn(g[-1]), col_fn(*g)))

    return [spec(tq, W_QG, lambda u: u, q_blk),
            spec(BLOCK, W_KV, lambda u: jnp.maximum(u * ratio - 1, 0), kv_blk),
            spec(tq, W_KV, lambda u: u, kv_blk),
            spec(BLOCK, W_KV, lambda u: jnp.minimum((u + 1) * ratio, nblk * ratio - 1), kv_blk)]


def _window_sink_attention(main, proj, bias, sink):
    B, S, _ = main.shape
    tq = min(512, S)
    nblk = S // tq
    lead_of = lambda b, h, u: (b,)
    in_specs = _band_specs(1, lead_of, tq, nblk, lambda b, h, u: M_QA // W_QG + h, lambda b, h, u: M_KVA // W_KV + h)
    in_specs += [pl.BlockSpec((1, GQA * BLOCK, BLOCK + 2 * WIN_A), lambda b, h, u: (h, 0, 0)),
                 pl.BlockSpec((1, GQA * BLOCK, LANES), lambda b, h, u: (h, 0, 0)),
                 pl.BlockSpec((None, tq, W_QG), lambda b, h, u: (b, u, OFF_GA // W_QG + h))]
    return pl.pallas_call(
        functools.partial(_band_kernel, nsub=tq // BLOCK, blk_axis=2, with_sink=True, halo=WIN_A),
        grid=(B, KV_A, nblk),
        in_specs=in_specs,
        out_specs=pl.BlockSpec((None, tq, W_QG), lambda b, h, u: (b, u, h)),
        out_shape=jax.ShapeDtypeStruct((B, S, H_A * HEAD_DIM), BF16),
        compiler_params=pltpu.CompilerParams(dimension_semantics=("parallel",) * 3, vmem_limit_bytes=VMEM_LIMIT),
        name="mixer_a",
    )(main, main, main, main, bias, sink, proj)


def _dilated_group_attention(x, bias, group):
    if group == 0:
        B, sub, _ = x.shape
        d, lead = 1, 1
        lead_of = lambda b, r, j, u: (b,)
        q0, kv0 = M_QC0 // W_QG, M_KVC0 // W_KV
        out_shape = jax.ShapeDtypeStruct((B, sub, W_C), F32)
    else:
        B, d, sub, _ = x.shape
        lead = 2
        lead_of = lambda b, r, j, u: (b, r)
        q0, kv0 = 0, W_C // W_KV
        out_shape = jax.ShapeDtypeStruct((B, d, sub, W_C), F32)
    tq = min(512, sub)
    nblk = sub // tq
    in_specs = _band_specs(lead, lead_of, tq, nblk, lambda b, r, j, u: q0 + j, lambda b, r, j, u: kv0 + j)
    in_specs += [pl.BlockSpec((1, GQA * BLOCK, BLOCK + 2 * C_RADIUS), lambda b, r, j, u: (j, 0, 0))]
    out_spec = pl.BlockSpec((None,) * lead + (tq, W_QG), lambda b, r, j, u: lead_of(b, r, j, u) + (u, j))
    return pl.pallas_call(
        functools.partial(_band_kernel, nsub=tq // BLOCK, blk_axis=3, with_sink=False, halo=C_RADIUS),
        grid=(B, d, KV_C, nblk),
        in_specs=in_specs,
        out_specs=[out_spec, out_spec],
        out_shape=[out_shape, out_shape],
        compiler_params=pltpu.CompilerParams(dimension_semantics=("parallel",) * 4, vmem_limit_bytes=VMEM_LIMIT),
        name=f"mixer_c{group}",
    )(x, x, x, x, bias)


def _merge_kernel(o0_ref, l0_ref, o1_ref, l1_ref, o2_ref, l2_ref, gate_ref, out_ref, stage_ref, *, tm):
    nk = W_C // LANES
    for a, (ref, g) in enumerate(((o1_ref, 1), (l1_ref, 1), (o2_ref, 2), (l2_ref, 2))):
        d = C_PATTERNS[g][1]
        for r in range(d):
            for k in range(nk):
                stage_ref[a * nk + k, pl.ds(r, tm // d, stride=d), :] = ref[0, r, :, k * LANES:(k + 1) * LANES]
    for k in range(nk):
        cols = slice(k * LANES, (k + 1) * LANES)
        o0, l0 = o0_ref[0, :, cols], l0_ref[0, :, cols]
        o1, l1, o2, l2 = stage_ref[k], stage_ref[nk + k], stage_ref[2 * nk + k], stage_ref[3 * nk + k]
        m = jnp.maximum(jnp.maximum(l0, l1), l2)
        w0, w1, w2 = jnp.exp2(l0 - m), jnp.exp2(l1 - m), jnp.exp2(l2 - m)
        y = (w0 * o0 + w1 * o1 + w2 * o2) / (w0 + w1 + w2)
        out_ref[0, :, cols] = (y * _silu(gate_ref[0, :, cols].astype(F32))).astype(out_ref.dtype)


def _merge_groups(outs, lses, proj, *, tm=256):
    B, S, _ = proj.shape
    d1, d2 = C_PATTERNS[1][1], C_PATTERNS[2][1]
    tok = pl.BlockSpec((1, tm, W_C), lambda b, i: (b, i, 0))
    res1 = pl.BlockSpec((1, d1, tm // d1, W_C), lambda b, i: (b, 0, i, 0))
    res2 = pl.BlockSpec((1, d2, tm // d2, W_C), lambda b, i: (b, 0, i, 0))
    return pl.pallas_call(
        functools.partial(_merge_kernel, tm=tm),
        grid=(B, S // tm),
        in_specs=[tok, tok, res1, res1, res2, res2, pl.BlockSpec((1, tm, W_C), lambda b, i: (b, i, OFF_GC // W_C))],
        out_specs=tok,
        out_shape=jax.ShapeDtypeStruct((B, S, W_C), BF16),
        scratch_shapes=[pltpu.VMEM((4 * (W_C // LANES), tm, LANES), F32)],
        compiler_params=pltpu.CompilerParams(dimension_semantics=("parallel", "parallel"),
                                             vmem_limit_bytes=VMEM_LIMIT),
        name="mixer_c_merge",
    )(outs[0], lses[0], outs[1], lses[1], outs[2], lses[2], proj)


def _global_kernel(q_ref, kv_ref, gate_ref, o_ref, v_s, *, seq, tk, prep_rows):
    @pl.when(pl.program_id(2) == 0)
    def _stage_values():
        def body(c, carry):
            rows = pl.ds(pl.multiple_of(c * prep_rows, prep_rows), prep_rows)
            v_s[rows, :] = jnp.concatenate([kv_ref[rows, HEAD_DIM:], jnp.ones((prep_rows, HEAD_DIM), BF16)], axis=1)
            return carry

        lax.fori_loop(0, seq // prep_rows, body, 0)

    q4 = _stack_heads(q_ref[...])
    rows4 = q4.shape[0]
    m = jnp.full((rows4, 1), NEG_INF, F32)
    acc = jnp.zeros((rows4, 2 * HEAD_DIM), F32)
    for c in range(seq // tk):
        rows = slice(c * tk, (c + 1) * tk)
        s = _nt_dot(q4, kv_ref[rows, :HEAD_DIM])
        m_new = jnp.maximum(m, jnp.max(s, axis=-1, keepdims=True))
        p = jnp.exp2(s - m_new).astype(BF16)
        acc = jnp.exp2(m - m_new) * acc + jnp.dot(p, v_s[rows, :], preferred_element_type=F32)
        m = m_new
    o = acc[:, :HEAD_DIM] / acc[:, HEAD_DIM:HEAD_DIM + 1]
    g = gate_ref[...].astype(F32)
    o_ref[...] = (_unstack_heads(o) * _silu(g)).astype(o_ref.dtype)


def _global_axial_attention(main, proj, *, tq=512, tk=512):
    B, S, _ = main.shape
    return pl.pallas_call(
        functools.partial(_global_kernel, seq=S, tk=tk, prep_rows=512),
        grid=(B, KV_B, S // tq),
        in_specs=[pl.BlockSpec((None, tq, W_QG), lambda b, h, i: (b, i, M_QB // W_QG + h)),
                  pl.BlockSpec((None, S, W_KV), lambda b, h, i: (b, 0, M_KVB // W_KV + h)),
                  pl.BlockSpec((None, tq, W_QG), lambda b, h, i: (b, i, OFF_GB // W_QG + h))],
        out_specs=pl.BlockSpec((None, tq, W_QG), lambda b, h, i: (b, i, h)),
        out_shape=jax.ShapeDtypeStruct((B, S, H_B * HEAD_DIM), BF16),
        scratch_shapes=[pltpu.VMEM((S, 2 * HEAD_DIM), BF16)],
        compiler_params=pltpu.CompilerParams(dimension_semantics=("parallel", "parallel", "arbitrary"),
                                             vmem_limit_bytes=VMEM_LIMIT),
        name="mixer_b",
    )(main, main, proj)


def _outproj_kernel(x_ref, a_ref, b_ref, c_ref, wa_ref, wb_ref, wc_ref, o_ref):
    y = jnp.dot(a_ref[...], wa_ref[...], preferred_element_type=F32)
    y += jnp.dot(b_ref[...], wb_ref[...], preferred_element_type=F32)
    y += jnp.dot(c_ref[...], wc_ref[...], preferred_element_type=F32)
    o_ref[...] = x_ref[...] + y


def _outproj(x2, mix_a, mix_b, mix_c, w, *, tm=512, tn=D_MODEL):
    T = x2.shape[0]
    ka, kb, kc = H_A * HEAD_DIM, H_B * HEAD_DIM, W_C
    return pl.pallas_call(
        _outproj_kernel,
        grid=(T // tm, D_MODEL // tn),
        in_specs=[pl.BlockSpec((tm, tn), lambda i, j: (i, j)),
                  pl.BlockSpec((tm, ka), lambda i, j: (i, 0)),
                  pl.BlockSpec((tm, kb), lambda i, j: (i, 0)),
                  pl.BlockSpec((tm, kc), lambda i, j: (i, 0)),
                  pl.BlockSpec((ka, tn), lambda i, j: (0, j)),
                  pl.BlockSpec((kb, tn), lambda i, j: (1, j)),
                  pl.BlockSpec((kc, tn), lambda i, j: ((ka + kb) // kc, j))],
        out_specs=pl.BlockSpec((tm, tn), lambda i, j: (i, j)),
        out_shape=jax.ShapeDtypeStruct((T, D_MODEL), F32),
        compiler_params=pltpu.CompilerParams(dimension_semantics=("parallel", "parallel"),
                                             vmem_limit_bytes=VMEM_LIMIT),
        name="outproj",
    )(x2, mix_a, mix_b, mix_c, w, w, w)


def _rope_table(S):
    rows = S // GRID_W
    row = jnp.repeat(jnp.arange(rows, dtype=jnp.int32), GRID_W)
    col = jnp.arange(S, dtype=jnp.int32) % GRID_W
    n_freq = HEAD_DIM // 4
    inv_freq = ROPE_THETA ** (-jnp.arange(n_freq, dtype=F32) / n_freq)
    ang_row = row.astype(F32)[:, None] * inv_freq[None, :]
    ang_col = col.astype(F32)[:, None] * inv_freq[None, :]
    cr, sr, cc, sc = jnp.cos(ang_row), jnp.sin(ang_row), jnp.cos(ang_col), jnp.sin(ang_col)
    cos = jnp.concatenate([cr, cr, cc, cc], axis=1)
    sin_signed = jnp.concatenate([-sr, sr, -sc, sc], axis=1)
    one, zero = jnp.ones_like(cos), jnp.zeros_like(cos)
    return jnp.concatenate([cos, cos, sin_signed, sin_signed, cos, one, sin_signed, zero], axis=1)


def _trunk(x, p):
    B, S, _ = x.shape
    T = B * S
    rope = _rope_table(S)
    x2 = x.reshape(T, D_MODEL)
    for l in range(DEPTH):
        proj = _inproj(x2, p["ln_g"][l], p["w_in"][l]).reshape(B, S, D_IN_PROJ)
        main, xc1, xc2 = _prepare(proj, p["wvec"][l], rope, p["seg_ones"])
        mix_a = _window_sink_attention(main, proj, p["bias_a"], p["sink_a"][l])
        mix_b = _global_axial_attention(main, proj)
        outs, lses = zip(*[_dilated_group_attention(xg, p["bias_c"][g], g) for g, xg in enumerate((main, xc1, xc2))])
        mix_c = _merge_groups(outs, lses, proj)
        x2 = _outproj(x2, mix_a.reshape(T, -1), mix_b.reshape(T, -1), mix_c.reshape(T, -1), p["w_out"][l])
    return x2.reshape(B, S, D_MODEL)


def _prepare_params(ln_g, w_in, q_norm_a, k_norm_a, sink_a, q_norm_b, k_norm_b, q_norm_c, k_norm_c, rel_bias, w_out):
    heads_c = [list(range(H_A + g * H_C, H_A + (g + 1) * H_C)) for g in range(N_C_GROUPS)]
    one = jnp.ones((DEPTH, HEAD_DIM), F32)

    def q_cols(w, n):
        return jnp.tile(w.astype(F32) * Q_SCALE, (1, n))

    def kv_cols(w, n):
        return jnp.tile(jnp.concatenate([w.astype(F32), one], axis=1), (1, n))

    wvec = jnp.concatenate([q_cols(q_norm_a, H_A), q_cols(q_norm_b, H_B), q_cols(q_norm_c, N_C_GROUPS * H_C),
                            kv_cols(k_norm_a, KV_A), kv_cols(k_norm_b, KV_B), kv_cols(k_norm_c, N_C_GROUPS * KV_C)],
                           axis=1)
    seg = np.arange(LANES) // HEAD_DIM
    return {
        "ln_g": ln_g.reshape(DEPTH, 1, D_MODEL),
        "w_in": _permute_columns(w_in, _proj_column_order()).astype(BF16),
        "w_out": w_out.astype(BF16),
        "wvec": wvec.reshape(DEPTH, 1, W_QKV),
        "seg_ones": jnp.asarray(seg[:, None] == seg[None, :], BF16),
        "sink_a": jnp.broadcast_to(jnp.repeat(sink_a.astype(F32) * LOG2E, BLOCK, axis=1)[:, :, None],
                                   (DEPTH, H_A * BLOCK, LANES)).reshape(DEPTH, KV_A, GQA * BLOCK, LANES),
        "bias_a": _band_bias(rel_bias, list(range(H_A)), 1, WIN_A, WIN_A),
        "bias_c": [_band_bias(rel_bias, heads_c[g], d, C_RADIUS, C_RADIUS) for g, (w, d) in enumerate(C_PATTERNS)],
    }


def kernel(x_prompt, x_sample, ln_g, w_in, q_norm_a, k_norm_a, sink_a, q_norm_b, k_norm_b,
           q_norm_c, k_norm_c, rel_bias, w_out):
    p = _prepare_params(ln_g, w_in, q_norm_a, k_norm_a, sink_a, q_norm_b, k_norm_b, q_norm_c, k_norm_c,
                        rel_bias, w_out)
    return _trunk(x_prompt, p), _trunk(x_sample, p)
```

```python
import functools
import math

import numpy as np
import jax
import jax.numpy as jnp
from jax import lax
from jax.experimental import pallas as pl
from jax.experimental.pallas import tpu as pltpu

F32 = jnp.float32
BF16 = jnp.bfloat16

D_MODEL = 2048
DEPTH = 2
HEAD_DIM = 64
N_HEADS_TOTAL = D_MODEL // HEAD_DIM
H_C = N_HEADS_TOTAL // 4
H_A = (N_HEADS_TOTAL - H_C) // 2
H_B = N_HEADS_TOTAL - H_C - H_A
KV_A = H_A // 4
KV_B = H_B // 4
KV_C = H_C // 4
GQA = 4
C_PATTERNS = ((128, 1), (512, 4), (2048, 16))
N_C_GROUPS = len(C_PATTERNS)
C_RADIUS = 64
assert all(w // (2 * d) == C_RADIUS for w, d in C_PATTERNS)
BLOCK = 128
LANES = 128
WIN_A = 128
GRID_W = 64
ROPE_THETA = 10000.0
T5_BUCKETS = 32
T5_MAX_DISTANCE = 1024
RMS_EPS = 1e-6
NEG_INF = -1e30
LOG2E = math.log2(math.e)
Q_SCALE = HEAD_DIM ** -0.5 * LOG2E

W_QG = GQA * HEAD_DIM
W_KV = 2 * HEAD_DIM
W_C = H_C * HEAD_DIM

OFF_QA = 0
OFF_QB = OFF_QA + H_A * HEAD_DIM
OFF_QC = OFF_QB + H_B * HEAD_DIM
OFF_KVA = OFF_QC + N_C_GROUPS * W_C
OFF_KVB = OFF_KVA + KV_A * W_KV
OFF_KVC = OFF_KVB + KV_B * W_KV
W_QKV = OFF_KVC + N_C_GROUPS * KV_C * W_KV
OFF_GA = W_QKV
OFF_GB = OFF_GA + H_A * HEAD_DIM
OFF_GC = OFF_GB + H_B * HEAD_DIM
D_IN_PROJ = OFF_GC + W_C

M_QA = 0
M_QB = M_QA + H_A * HEAD_DIM
M_QC0 = M_QB + H_B * HEAD_DIM
M_KVA = M_QC0 + W_C
M_KVB = M_KVA + KV_A * W_KV
M_KVC0 = M_KVB + KV_B * W_KV
W_MAIN = M_KVC0 + KV_C * W_KV
W_XC = W_C + KV_C * W_KV

VMEM_LIMIT = 48 * 1024 * 1024
BAND_ROWS = 1024


def _proj_column_order():
    sizes = (H_A * 64, KV_A * 64, KV_A * 64, H_A * 64, H_B * 64, KV_B * 64, KV_B * 64, H_B * 64,
             N_C_GROUPS * H_C * 64, N_C_GROUPS * KV_C * 64, N_C_GROUPS * KV_C * 64, H_C * 64)
    o = np.concatenate([[0], np.cumsum(sizes)])
    qa, ka, va, ga, qb, kb, vb, gb, qc, kc, vc, gc = (np.arange(o[i], o[i + 1]) for i in range(12))

    def kv_pairs(k, v, n):
        return np.concatenate([np.concatenate([k[h * 64:(h + 1) * 64], v[h * 64:(h + 1) * 64]]) for h in range(n)])

    order = np.concatenate([qa, qb, qc, kv_pairs(ka, va, KV_A), kv_pairs(kb, vb, KV_B),
                            kv_pairs(kc, vc, N_C_GROUPS * KV_C), ga, gb, gc])
    assert order.shape == (D_IN_PROJ,) and np.array_equal(np.sort(order), np.arange(D_IN_PROJ))
    return order


def _permute_columns(w, order):
    cuts = np.flatnonzero(np.diff(order) != 1) + 1
    starts = np.concatenate([[0], cuts])
    stops = np.concatenate([cuts, [order.size]])
    return jnp.concatenate([w[..., int(order[a]):int(order[b - 1]) + 1] for a, b in zip(starts, stops)], axis=-1)


def _t5_bucket(rel):
    half = T5_BUCKETS // 2
    max_exact = half // 2
    n = np.abs(rel)
    log_ratio = np.log(np.maximum(n, 1).astype(np.float64) / max_exact) / math.log(T5_MAX_DISTANCE / max_exact)
    large = np.minimum(max_exact + (log_ratio * (half - max_exact)).astype(np.int32), half - 1)
    return np.where(rel > 0, half, 0) + np.where(n < max_exact, n, large)


def _band_bias(rel_bias, heads, dilation, radius, halo):
    width = BLOCK + 2 * halo
    n = width + BLOCK - 1
    rel = np.arange(n) - (halo + BLOCK - 1)
    tab = rel_bias[_t5_bucket(rel * dilation)][:, heads[0]:heads[0] + len(heads)].astype(F32) * LOG2E
    tab = jnp.where((np.abs(rel) <= radius)[:, None], tab, NEG_INF)
    stream = jnp.tile(jnp.pad(tab, ((0, 1), (0, 0))), (BLOCK, 1))[:BLOCK * n]
    b = stream.reshape(BLOCK, n, len(heads))[:, BLOCK - 1:, :]
    return b.transpose(2, 0, 1).reshape(len(heads) // GQA, GQA * BLOCK, width)


def _rms(x, w):
    return x * lax.rsqrt(jnp.mean(x * x, axis=-1, keepdims=True) + RMS_EPS) * w


def _stack_heads(x):
    return jnp.concatenate([x[:, HEAD_DIM * r:HEAD_DIM * (r + 1)] for r in range(GQA)], axis=0)


def _unstack_heads(x):
    rows = x.shape[0] // GQA
    return jnp.concatenate([x[rows * r:rows * (r + 1)] for r in range(GQA)], axis=1)


def _silu(g):
    return g * jax.nn.sigmoid(g)


def _nt_dot(a, b):
    return lax.dot_general(a, b, (((1,), (1,)), ((), ())), preferred_element_type=F32)


def _inproj_kernel(x_ref, g_ref, w_ref, o_ref, h_ref):
    @pl.when(pl.program_id(1) == 0)
    def _():
        x = x_ref[...]
        h_ref[...] = _rms(x, g_ref[...]).astype(BF16)

    o_ref[...] = jnp.dot(h_ref[...], w_ref[...], preferred_element_type=F32).astype(o_ref.dtype)


def _inproj(x2, g, w, *, tm=1024, tn=512):
    T = x2.shape[0]
    return pl.pallas_call(
        _inproj_kernel,
        grid=(T // tm, D_IN_PROJ // tn),
        in_specs=[pl.BlockSpec((tm, D_MODEL), lambda i, j: (i, 0)),
                  pl.BlockSpec((1, D_MODEL), lambda i, j: (0, 0)),
                  pl.BlockSpec((D_MODEL, tn), lambda i, j: (0, j))],
        out_specs=pl.BlockSpec((tm, tn), lambda i, j: (i, j)),
        out_shape=jax.ShapeDtypeStruct((T, D_IN_PROJ), BF16),
        scratch_shapes=[pltpu.VMEM((tm, D_MODEL), BF16)],
        compiler_params=pltpu.CompilerParams(dimension_semantics=("parallel", "arbitrary"),
                                             vmem_limit_bytes=VMEM_LIMIT),
        name="inproj",
    )(x2, g, w)


def _prep_plan():
    plan = []
    for cb in range(W_QKV // LANES):
        col = cb * LANES
        if col < OFF_QB:
            plan.append((False, False, "main", M_QA + col - OFF_QA))
        elif col < OFF_QC:
            plan.append((False, True, "main", M_QB + col - OFF_QB))
        elif col < OFF_KVA:
            g, off = divmod(col - OFF_QC, W_C)
            plan.append((False, False, "main", M_QC0 + off) if g == 0 else (False, False, f"xc{g}", off))
        elif col < OFF_KVB:
            plan.append((True, False, "main", M_KVA + col - OFF_KVA))
        elif col < OFF_KVC:
            plan.append((True, True, "main", M_KVB + col - OFF_KVB))
        else:
            g, off = divmod(col - OFF_KVC, KV_C * W_KV)
            plan.append((True, False, "main", M_KVC0 + off) if g == 0 else (True, False, f"xc{g}", W_C + off))
    return plan


def _prep_kernel(x_ref, w_ref, rope_ref, ones_ref, main_ref, xc1_ref, xc2_ref, stage_ref, *, tp, plan):
    lane = lax.broadcasted_iota(jnp.int32, (tp, LANES), 1)
    is_k = lane < HEAD_DIM
    swap_hi = (lane & 16) != 0
    seg_ones = ones_ref[...]
    n_stage = W_XC // LANES
    for cb, (is_kv, rotary, dest, col) in enumerate(plan):
        cols = slice(cb * LANES, (cb + 1) * LANES)
        x = x_ref[0, :, cols].astype(F32)
        ssq = jnp.dot((x * x).astype(BF16), seg_ones, preferred_element_type=F32)
        inv = lax.rsqrt(ssq * (1.0 / HEAD_DIM) + RMS_EPS)
        if is_kv:
            inv = jnp.where(is_k, inv, 1.0)
        y = x * inv * w_ref[:, cols]
        if rotary:
            base = 2 * LANES if is_kv else 0
            cos = rope_ref[:, base:base + LANES]
            sin_signed = rope_ref[:, base + LANES:base + 2 * LANES]
            swapped = jnp.where(swap_hi, pltpu.roll(y, 16, 1), pltpu.roll(y, LANES - 16, 1))
            y = y * cos + swapped * sin_signed
        if dest == "main":
            main_ref[0, :, col:col + LANES] = y.astype(BF16)
        else:
            stage_ref[(int(dest[2]) - 1) * n_stage + col // LANES] = y
    for g, xc_ref in ((1, xc1_ref), (2, xc2_ref)):
        d = C_PATTERNS[g][1]
        for r in range(d):
            for k in range(n_stage):
                piece = stage_ref[(g - 1) * n_stage + k, pl.ds(r, tp // d, stride=d), :]
                xc_ref[0, r, :, k * LANES:(k + 1) * LANES] = piece.astype(BF16)


def _prepare(proj, wvec, rope, seg_ones, *, tp=256):
    B, S, _ = proj.shape
    d1, d2 = C_PATTERNS[1][1], C_PATTERNS[2][1]
    plan = _prep_plan()
    return pl.pallas_call(
        functools.partial(_prep_kernel, tp=tp, plan=plan),
        grid=(B, S // tp),
        in_specs=[pl.BlockSpec((1, tp, W_QKV), lambda b, i: (b, i, 0)),
                  pl.BlockSpec((1, W_QKV), lambda b, i: (0, 0)),
                  pl.BlockSpec((tp, 4 * LANES), lambda b, i: (i, 0)),
                  pl.BlockSpec((LANES, LANES), lambda b, i: (0, 0))],
        out_specs=[pl.BlockSpec((1, tp, W_MAIN), lambda b, i: (b, i, 0)),
                   pl.BlockSpec((1, d1, tp // d1, W_XC), lambda b, i: (b, 0, i, 0)),
                   pl.BlockSpec((1, d2, tp // d2, W_XC), lambda b, i: (b, 0, i, 0))],
        out_shape=[jax.ShapeDtypeStruct((B, S, W_MAIN), BF16),
                   jax.ShapeDtypeStruct((B, d1, S // d1, W_XC), BF16),
                   jax.ShapeDtypeStruct((B, d2, S // d2, W_XC), BF16)],
        scratch_shapes=[pltpu.VMEM((2 * (W_XC // LANES), tp, LANES), F32)],
        compiler_params=pltpu.CompilerParams(dimension_semantics=("parallel", "parallel"),
                                             vmem_limit_bytes=VMEM_LIMIT),
        name="prepare",
    )(proj, wvec, rope, seg_ones)


def _band_kernel(*refs, nsub, blk_axis, with_sink, halo):
    if with_sink:
        q_ref, kp_ref, kc_ref, kn_ref, bias_ref, sink_ref, gate_ref, o_ref = refs
    else:
        q_ref, kp_ref, kc_ref, kn_ref, bias_ref, o_ref, l_ref = refs
    i = pl.program_id(blk_axis)
    nb = pl.num_programs(blk_axis)
    kv = jnp.concatenate([kp_ref[...], kc_ref[...], kn_ref[...]], axis=0)
    bias = bias_ref[0]
    width = BLOCK + 2 * halo
    col = lax.broadcasted_iota(jnp.int32, bias.shape, 1)
    for t in range(nsub):
        rows = slice(t * BLOCK, (t + 1) * BLOCK)
        q4 = _stack_heads(q_ref[rows, :])
        start = (t + 1) * BLOCK - halo
        win = kv[start:start + width]
        s = _nt_dot(q4, win[:, :HEAD_DIM]) + bias
        if t == 0:
            s = jnp.where((col < halo) & (i == 0), NEG_INF, s)
        if t == nsub - 1:
            s = jnp.where((col >= BLOCK + halo) & (i == nb - 1), NEG_INF, s)
        m = jnp.broadcast_to(jnp.max(s, axis=-1, keepdims=True), (GQA * BLOCK, LANES))
        if with_sink:
            sink = sink_ref[0]
            m = jnp.maximum(m, sink)
        e = jnp.exp2(s - jnp.concatenate([m] * (width // LANES), axis=1)).astype(BF16)
        pv = jnp.dot(e, jnp.concatenate([win, jnp.ones((width, LANES), BF16)], axis=1), preferred_element_type=F32)
        denom = pv[:, LANES:]
        if with_sink:
            denom = denom + jnp.exp2(sink - m)
        o = pv[:, :LANES] / denom
        low = lax.broadcasted_iota(jnp.int32, (BLOCK, LANES), 1) < HEAD_DIM

        def pair_heads(x, p):
            even, odd = x[2 * p * BLOCK:(2 * p + 1) * BLOCK], x[(2 * p + 1) * BLOCK:(2 * p + 2) * BLOCK]
            return even, odd

        pairs = []
        for p in range(GQA // 2):
            even, odd = pair_heads(o, p)
            pairs.append(jnp.where(low, pltpu.roll(even, HEAD_DIM, 1), odd))
        o2 = jnp.concatenate(pairs, axis=1)
        if with_sink:
            g = gate_ref[rows, :].astype(F32)
            o_ref[rows, :] = (o2 * _silu(g)).astype(o_ref.dtype)
        else:
            o_ref[rows, :] = o2
            lse = m + jnp.log2(denom)
            l_ref[rows, :] = jnp.concatenate([jnp.where(low, *pair_heads(lse, p)) for p in range(GQA // 2)], axis=1)


def _band_specs(lead, lead_of, tq, nblk, q_blk, kv_blk):
    ratio = tq // BLOCK
    none = (None,) * lead

    def spec(rows, width, row_fn, col_fn):
        return pl.BlockSpec(none + (rows, width), lambda *g: lead_of(*g) + (row_fn(g[-1]), col_fn(*g)))

    return [spec(tq, W_QG, lambda u: u, q_blk),
            spec(BLOCK, W_KV, lambda u: jnp.maximum(u * ratio - 1, 0), kv_blk),
            spec(tq, W_KV, lambda u: u, kv_blk),
            spec(BLOCK, W_KV, lambda u: jnp.minimum((u + 1) * ratio, nblk * ratio - 1), kv_blk)]


def _window_sink_attention(main, proj, bias, sink):
    B, S, _ = main.shape
    tq = min(BAND_ROWS, S)
    nblk = S // tq
    lead_of = lambda b, h, u: (b,)
    in_specs = _band_specs(1, lead_of, tq, nblk, lambda b, h, u: M_QA // W_QG + h, lambda b, h, u: M_KVA // W_KV + h)
    in_specs += [pl.BlockSpec((1, GQA * BLOCK, BLOCK + 2 * WIN_A), lambda b, h, u: (h, 0, 0)),
                 pl.BlockSpec((1, GQA * BLOCK, LANES), lambda b, h, u: (h, 0, 0)),
                 pl.BlockSpec((None, tq, W_QG), lambda b, h, u: (b, u, OFF_GA // W_QG + h))]
    return pl.pallas_call(
        functools.partial(_band_kernel, nsub=tq // BLOCK, blk_axis=2, with_sink=True, halo=WIN_A),
        grid=(B, KV_A, nblk),
        in_specs=in_specs,
        out_specs=pl.BlockSpec((None, tq, W_QG), lambda b, h, u: (b, u, h)),
        out_shape=jax.ShapeDtypeStruct((B, S, H_A * HEAD_DIM), BF16),
        compiler_params=pltpu.CompilerParams(dimension_semantics=("parallel",) * 3, vmem_limit_bytes=VMEM_LIMIT),
        name="mixer_a",
    )(main, main, main, main, bias, sink, proj)


def _dilated_group_attention(x, bias, group):
    if group == 0:
        B, sub, _ = x.shape
        d, lead = 1, 1
        lead_of = lambda b, r, j, u: (b,)
        q0, kv0 = M_QC0 // W_QG, M_KVC0 // W_KV
        out_shape = jax.ShapeDtypeStruct((B, sub, W_C), F32)
    else:
        B, d, sub, _ = x.shape
        lead = 2
        lead_of = lambda b, r, j, u: (b, r)
        q0, kv0 = 0, W_C // W_KV
        out_shape = jax.ShapeDtypeStruct((B, d, sub, W_C), F32)
    tq = min(BAND_ROWS, sub)
    nblk = sub // tq
    in_specs = _band_specs(lead, lead_of, tq, nblk, lambda b, r, j, u: q0 + j, lambda b, r, j, u: kv0 + j)
    in_specs += [pl.BlockSpec((1, GQA * BLOCK, BLOCK + 2 * C_RADIUS), lambda b, r, j, u: (j, 0, 0))]
    out_spec = pl.BlockSpec((None,) * lead + (tq, W_QG), lambda b, r, j, u: lead_of(b, r, j, u) + (u, j))
    return pl.pallas_call(
        functools.partial(_band_kernel, nsub=tq // BLOCK, blk_axis=3, with_sink=False, halo=C_RADIUS),
        grid=(B, d, KV_C, nblk),
        in_specs=in_specs,
        out_specs=[out_spec, out_spec],
        out_shape=[out_shape, out_shape],
        compiler_params=pltpu.CompilerParams(dimension_semantics=("parallel",) * 4, vmem_limit_bytes=VMEM_LIMIT),
        name=f"mixer_c{group}",
    )(x, x, x, x, bias)


def _merge_kernel(o0_ref, l0_ref, o1_ref, l1_ref, o2_ref, l2_ref, gate_ref, out_ref, stage_ref, *, tm):
    nk = W_C // LANES
    for a, (ref, g) in enumerate(((o1_ref, 1), (l1_ref, 1), (o2_ref, 2), (l2_ref, 2))):
        d = C_PATTERNS[g][1]
        for r in range(d):
            for k in range(nk):
                stage_ref[a * nk + k, pl.ds(r, tm // d, stride=d), :] = ref[0, r, :, k * LANES:(k + 1) * LANES]
    for k in range(nk):
        cols = slice(k * LANES, (k + 1) * LANES)
        o0, l0 = o0_ref[0, :, cols], l0_ref[0, :, cols]
        o1, l1, o2, l2 = stage_ref[k], stage_ref[nk + k], stage_ref[2 * nk + k], stage_ref[3 * nk + k]
        m = jnp.maximum(jnp.maximum(l0, l1), l2)
        w0, w1, w2 = jnp.exp2(l0 - m), jnp.exp2(l1 - m), jnp.exp2(l2 - m)
        y = (w0 * o0 + w1 * o1 + w2 * o2) / (w0 + w1 + w2)
        out_ref[0, :, cols] = (y * _silu(gate_ref[0, :, cols].astype(F32))).astype(out_ref.dtype)


def _merge_groups(outs, lses, proj, *, tm=1024):
    B, S, _ = proj.shape
    d1, d2 = C_PATTERNS[1][1], C_PATTERNS[2][1]
    tok = pl.BlockSpec((1, tm, W_C), lambda b, i: (b, i, 0))
    res1 = pl.BlockSpec((1, d1, tm // d1, W_C), lambda b, i: (b, 0, i, 0))
    res2 = pl.BlockSpec((1, d2, tm // d2, W_C), lambda b, i: (b, 0, i, 0))
    return pl.pallas_call(
        functools.partial(_merge_kernel, tm=tm),
        grid=(B, S // tm),
        in_specs=[tok, tok, res1, res1, res2, res2, pl.BlockSpec((1, tm, W_C), lambda b, i: (b, i, OFF_GC // W_C))],
        out_specs=tok,
        out_shape=jax.ShapeDtypeStruct((B, S, W_C), BF16),
        scratch_shapes=[pltpu.VMEM((4 * (W_C // LANES), tm, LANES), F32)],
        compiler_params=pltpu.CompilerParams(dimension_semantics=("parallel", "parallel"),
                                             vmem_limit_bytes=VMEM_LIMIT),
        name="mixer_c_merge",
    )(outs[0], lses[0], outs[1], lses[1], outs[2], lses[2], proj)


def _global_kernel(q_ref, kv_ref, gate_ref, o_ref, v_s, *, seq, tk, prep_rows):
    @pl.when(pl.program_id(2) == 0)
    def _stage_values():
        def body(c, carry):
            rows = pl.ds(pl.multiple_of(c * prep_rows, prep_rows), prep_rows)
            v_s[rows, :] = jnp.concatenate([kv_ref[rows, HEAD_DIM:], jnp.ones((prep_rows, HEAD_DIM), BF16)], axis=1)
            return carry

        lax.fori_loop(0, seq // prep_rows, body, 0)

    q4 = _stack_heads(q_ref[...])
    rows4 = q4.shape[0]
    m = jnp.full((rows4, 1), NEG_INF, F32)
    acc = jnp.zeros((rows4, 2 * HEAD_DIM), F32)
    for c in range(seq // tk):
        rows = slice(c * tk, (c + 1) * tk)
        s = _nt_dot(q4, kv_ref[rows, :HEAD_DIM])
        m_new = jnp.maximum(m, jnp.max(s, axis=-1, keepdims=True))
        p = jnp.exp2(s - m_new).astype(BF16)
        acc = jnp.exp2(m - m_new) * acc + jnp.dot(p, v_s[rows, :], preferred_element_type=F32)
        m = m_new
    o = acc[:, :HEAD_DIM] / acc[:, HEAD_DIM:HEAD_DIM + 1]
    g = gate_ref[...].astype(F32)
    o_ref[...] = (_unstack_heads(o) * _silu(g)).astype(o_ref.dtype)


def _global_axial_attention(main, proj, *, tq=512, tk=512):
    B, S, _ = main.shape
    return pl.pallas_call(
        functools.partial(_global_kernel, seq=S, tk=tk, prep_rows=512),
        grid=(B, KV_B, S // tq),
        in_specs=[pl.BlockSpec((None, tq, W_QG), lambda b, h, i: (b, i, M_QB // W_QG + h)),
                  pl.BlockSpec((None, S, W_KV), lambda b, h, i: (b, 0, M_KVB // W_KV + h)),
                  pl.BlockSpec((None, tq, W_QG), lambda b, h, i: (b, i, OFF_GB // W_QG + h))],
        out_specs=pl.BlockSpec((None, tq, W_QG), lambda b, h, i: (b, i, h)),
        out_shape=jax.ShapeDtypeStruct((B, S, H_B * HEAD_DIM), BF16),
        scratch_shapes=[pltpu.VMEM((S, 2 * HEAD_DIM), BF16)],
        compiler_params=pltpu.CompilerParams(dimension_semantics=("parallel", "parallel", "arbitrary"),
                                             vmem_limit_bytes=VMEM_LIMIT),
        name="mixer_b",
    )(main, main, proj)


def _outproj_kernel(x_ref, a_ref, b_ref, c_ref, wa_ref, wb_ref, wc_ref, o_ref):
    y = jnp.dot(a_ref[...], wa_ref[...], preferred_element_type=F32)
    y += jnp.dot(b_ref[...], wb_ref[...], preferred_element_type=F32)
    y += jnp.dot(c_ref[...], wc_ref[...], preferred_element_type=F32)
    o_ref[...] = x_ref[...] + y


def _outproj(x2, mix_a, mix_b, mix_c, w, *, tm=512, tn=D_MODEL):
    T = x2.shape[0]
    ka, kb, kc = H_A * HEAD_DIM, H_B * HEAD_DIM, W_C
    return pl.pallas_call(
        _outproj_kernel,
        grid=(T // tm, D_MODEL // tn),
        in_specs=[pl.BlockSpec((tm, tn), lambda i, j: (i, j)),
                  pl.BlockSpec((tm, ka), lambda i, j: (i, 0)),
                  pl.BlockSpec((tm, kb), lambda i, j: (i, 0)),
                  pl.BlockSpec((tm, kc), lambda i, j: (i, 0)),
                  pl.BlockSpec((ka, tn), lambda i, j: (0, j)),
                  pl.BlockSpec((kb, tn), lambda i, j: (1, j)),
                  pl.BlockSpec((kc, tn), lambda i, j: ((ka + kb) // kc, j))],
        out_specs=pl.BlockSpec((tm, tn), lambda i, j: (i, j)),
        out_shape=jax.ShapeDtypeStruct((T, D_MODEL), F32),
        compiler_params=pltpu.CompilerParams(dimension_semantics=("parallel", "parallel"),
                                             vmem_limit_bytes=VMEM_LIMIT),
        name="outproj",
    )(x2, mix_a, mix_b, mix_c, w, w, w)


def _rope_table(S):
    rows = S // GRID_W
    row = jnp.repeat(jnp.arange(rows, dtype=jnp.int32), GRID_W)
    col = jnp.arange(S, dtype=jnp.int32) % GRID_W
    n_freq = HEAD_DIM // 4
    inv_freq = ROPE_THETA ** (-jnp.arange(n_freq, dtype=F32) / n_freq)
    ang_row = row.astype(F32)[:, None] * inv_freq[None, :]
    ang_col = col.astype(F32)[:, None] * inv_freq[None, :]
    cr, sr, cc, sc = jnp.cos(ang_row), jnp.sin(ang_row), jnp.cos(ang_col), jnp.sin(ang_col)
    cos = jnp.concatenate([cr, cr, cc, cc], axis=1)
    sin_signed = jnp.concatenate([-sr, sr, -sc, sc], axis=1)
    one, zero = jnp.ones_like(cos), jnp.zeros_like(cos)
    return jnp.concatenate([cos, cos, sin_signed, sin_signed, cos, one, sin_signed, zero], axis=1)


def _trunk(x, p):
    B, S, _ = x.shape
    T = B * S
    rope = _rope_table(S)
    x2 = x.reshape(T, D_MODEL)
    for l in range(DEPTH):
        proj = _inproj(x2, p["ln_g"][l], p["w_in"][l]).reshape(B, S, D_IN_PROJ)
        main, xc1, xc2 = _prepare(proj, p["wvec"][l], rope, p["seg_ones"])
        mix_a = _window_sink_attention(main, proj, p["bias_a"], p["sink_a"][l])
        mix_b = _global_axial_attention(main, proj)
        outs, lses = zip(*[_dilated_group_attention(xg, p["bias_c"][g], g) for g, xg in enumerate((main, xc1, xc2))])
        mix_c = _merge_groups(outs, lses, proj)
        x2 = _outproj(x2, mix_a.reshape(T, -1), mix_b.reshape(T, -1), mix_c.reshape(T, -1), p["w_out"][l])
    return x2.reshape(B, S, D_MODEL)


def _prepare_params(ln_g, w_in, q_norm_a, k_norm_a, sink_a, q_norm_b, k_norm_b, q_norm_c, k_norm_c, rel_bias, w_out):
    heads_c = [list(range(H_A + g * H_C, H_A + (g + 1) * H_C)) for g in range(N_C_GROUPS)]
    one = jnp.ones((DEPTH, HEAD_DIM), F32)

    def q_cols(w, n):
        return jnp.tile(w.astype(F32) * Q_SCALE, (1, n))

    def kv_cols(w, n):
        return jnp.tile(jnp.concatenate([w.astype(F32), one], axis=1), (1, n))

    wvec = jnp.concatenate([q_cols(q_norm_a, H_A), q_cols(q_norm_b, H_B), q_cols(q_norm_c, N_C_GROUPS * H_C),
                            kv_cols(k_norm_a, KV_A), kv_cols(k_norm_b, KV_B), kv_cols(k_norm_c, N_C_GROUPS * KV_C)],
                           axis=1)
    seg = np.arange(LANES) // HEAD_DIM
    return {
        "ln_g": ln_g.reshape(DEPTH, 1, D_MODEL),
        "w_in": _permute_columns(w_in, _proj_column_order()).astype(BF16),
        "w_out": w_out.astype(BF16),
        "wvec": wvec.reshape(DEPTH, 1, W_QKV),
        "seg_ones": jnp.asarray(seg[:, None] == seg[None, :], BF16),
        "sink_a": jnp.broadcast_to(jnp.repeat(sink_a.astype(F32) * LOG2E, BLOCK, axis=1)[:, :, None],
                                   (DEPTH, H_A * BLOCK, LANES)).reshape(DEPTH, KV_A, GQA * BLOCK, LANES),
        "bias_a": _band_bias(rel_bias, list(range(H_A)), 1, WIN_A, WIN_A),
        "bias_c": [_band_bias(rel_bias, heads_c[g], d, C_RADIUS, C_RADIUS) for g, (w, d) in enumerate(C_PATTERNS)],
    }


def kernel(x_prompt, x_sample, ln_g, w_in, q_norm_a, k_norm_a, sink_a, q_norm_b, k_norm_b,
           q_norm_c, k_norm_c, rel_bias, w_out):
    p = _prepare_params(ln_g, w_in, q_norm_a, k_norm_a, sink_a, q_norm_b, k_norm_b, q_norm_c, k_norm_c,
                        rel_bias, w_out)
    return _trunk(x_prompt, p), _trunk(x_sample, p)
```

```python
import functools
import math

import numpy as np
import jax
import jax.numpy as jnp
from jax import lax
from jax.experimental import pallas as pl
from jax.experimental.pallas import tpu as pltpu

F32 = jnp.float32
BF16 = jnp.bfloat16

D_MODEL = 2048
DEPTH = 2
HEAD_DIM = 64
N_HEADS_TOTAL = D_MODEL // HEAD_DIM
H_C = N_HEADS_TOTAL // 4
H_A = (N_HEADS_TOTAL - H_C) // 2
H_B = N_HEADS_TOTAL - H_C - H_A
KV_A = H_A // 4
KV_B = H_B // 4
KV_C = H_C // 4
GQA = 4
C_PATTERNS = ((128, 1), (512, 4), (2048, 16))
N_C_GROUPS = len(C_PATTERNS)
C_RADIUS = 64
assert all(w // (2 * d) == C_RADIUS for w, d in C_PATTERNS)
BLOCK = 128
LANES = 128
WIN_A = 128
GRID_W = 64
ROPE_THETA = 10000.0
T5_BUCKETS = 32
T5_MAX_DISTANCE = 1024
RMS_EPS = 1e-6
NEG_INF = -1e30
LOG2E = math.log2(math.e)
Q_SCALE = HEAD_DIM ** -0.5 * LOG2E

W_QG = GQA * HEAD_DIM
W_KV = 2 * HEAD_DIM
W_C = H_C * HEAD_DIM

OFF_QA = 0
OFF_QB = OFF_QA + H_A * HEAD_DIM
OFF_QC = OFF_QB + H_B * HEAD_DIM
OFF_KVA = OFF_QC + N_C_GROUPS * W_C
OFF_KVB = OFF_KVA + KV_A * W_KV
OFF_KVC = OFF_KVB + KV_B * W_KV
W_QKV = OFF_KVC + N_C_GROUPS * KV_C * W_KV
OFF_GA = W_QKV
OFF_GB = OFF_GA + H_A * HEAD_DIM
OFF_GC = OFF_GB + H_B * HEAD_DIM
D_IN_PROJ = OFF_GC + W_C

M_QA = 0
M_QB = M_QA + H_A * HEAD_DIM
M_QC0 = M_QB + H_B * HEAD_DIM
M_KVA = M_QC0 + W_C
M_KVB = M_KVA + KV_A * W_KV
M_KVC0 = M_KVB + KV_B * W_KV
W_MAIN = M_KVC0 + KV_C * W_KV
W_XC = W_C + KV_C * W_KV

VMEM_LIMIT = 48 * 1024 * 1024
BAND_ROWS = 1024


def _proj_column_order():
    sizes = (H_A * 64, KV_A * 64, KV_A * 64, H_A * 64, H_B * 64, KV_B * 64, KV_B * 64, H_B * 64,
             N_C_GROUPS * H_C * 64, N_C_GROUPS * KV_C * 64, N_C_GROUPS * KV_C * 64, H_C * 64)
    o = np.concatenate([[0], np.cumsum(sizes)])
    qa, ka, va, ga, qb, kb, vb, gb, qc, kc, vc, gc = (np.arange(o[i], o[i + 1]) for i in range(12))

    def kv_pairs(k, v, n):
        return np.concatenate([np.concatenate([k[h * 64:(h + 1) * 64], v[h * 64:(h + 1) * 64]]) for h in range(n)])

    order = np.concatenate([qa, qb, qc, kv_pairs(ka, va, KV_A), kv_pairs(kb, vb, KV_B),
                            kv_pairs(kc, vc, N_C_GROUPS * KV_C), ga, gb, gc])
    assert order.shape == (D_IN_PROJ,) and np.array_equal(np.sort(order), np.arange(D_IN_PROJ))
    return order


def _permute_columns(w, order):
    cuts = np.flatnonzero(np.diff(order) != 1) + 1
    starts = np.concatenate([[0], cuts])
    stops = np.concatenate([cuts, [order.size]])
    return jnp.concatenate([w[..., int(order[a]):int(order[b - 1]) + 1] for a, b in zip(starts, stops)], axis=-1)


def _t5_bucket(rel):
    half = T5_BUCKETS // 2
    max_exact = half // 2
    n = np.abs(rel)
    log_ratio = np.log(np.maximum(n, 1).astype(np.float64) / max_exact) / math.log(T5_MAX_DISTANCE / max_exact)
    large = np.minimum(max_exact + (log_ratio * (half - max_exact)).astype(np.int32), half - 1)
    return np.where(rel > 0, half, 0) + np.where(n < max_exact, n, large)


def _band_bias(rel_bias, heads, dilation, radius, halo):
    width = BLOCK + 2 * halo
    n = width + BLOCK - 1
    rel = np.arange(n) - (halo + BLOCK - 1)
    tab = rel_bias[_t5_bucket(rel * dilation)][:, heads[0]:heads[0] + len(heads)].astype(F32) * LOG2E
    tab = jnp.where((np.abs(rel) <= radius)[:, None], tab, NEG_INF)
    stream = jnp.tile(jnp.pad(tab, ((0, 1), (0, 0))), (BLOCK, 1))[:BLOCK * n]
    b = stream.reshape(BLOCK, n, len(heads))[:, BLOCK - 1:, :]
    return b.transpose(2, 0, 1).reshape(len(heads) // GQA, GQA * BLOCK, width)


def _rms(x, w):
    return x * lax.rsqrt(jnp.mean(x * x, axis=-1, keepdims=True) + RMS_EPS) * w


def _stack_heads(x):
    return jnp.concatenate([x[:, HEAD_DIM * r:HEAD_DIM * (r + 1)] for r in range(GQA)], axis=0)


def _unstack_heads(x):
    rows = x.shape[0] // GQA
    return jnp.concatenate([x[rows * r:rows * (r + 1)] for r in range(GQA)], axis=1)


def _silu(g):
    return g * jax.nn.sigmoid(g)


def _nt_dot(a, b):
    return lax.dot_general(a, b, (((1,), (1,)), ((), ())), preferred_element_type=F32)


def _inproj_kernel(x_ref, g_ref, w_ref, o_ref, h_ref):
    @pl.when(pl.program_id(1) == 0)
    def _():
        x = x_ref[...]
        h_ref[...] = _rms(x, g_ref[...]).astype(BF16)

    o_ref[...] = jnp.dot(h_ref[...], w_ref[...], preferred_element_type=F32).astype(o_ref.dtype)


def _inproj(x2, g, w, *, tm=1024, tn=512):
    T = x2.shape[0]
    return pl.pallas_call(
        _inproj_kernel,
        grid=(T // tm, D_IN_PROJ // tn),
        in_specs=[pl.BlockSpec((tm, D_MODEL), lambda i, j: (i, 0)),
                  pl.BlockSpec((1, D_MODEL), lambda i, j: (0, 0)),
                  pl.BlockSpec((D_MODEL, tn), lambda i, j: (0, j))],
        out_specs=pl.BlockSpec((tm, tn), lambda i, j: (i, j)),
        out_shape=jax.ShapeDtypeStruct((T, D_IN_PROJ), BF16),
        scratch_shapes=[pltpu.VMEM((tm, D_MODEL), BF16)],
        compiler_params=pltpu.CompilerParams(dimension_semantics=("parallel", "arbitrary"),
                                             vmem_limit_bytes=VMEM_LIMIT),
        name="inproj",
    )(x2, g, w)


def _prep_plan():
    plan = []
    for cb in range(W_QKV // LANES):
        col = cb * LANES
        if col < OFF_QB:
            plan.append((False, False, "main", M_QA + col - OFF_QA))
        elif col < OFF_QC:
            plan.append((False, True, "main", M_QB + col - OFF_QB))
        elif col < OFF_KVA:
            g, off = divmod(col - OFF_QC, W_C)
            plan.append((False, False, "main", M_QC0 + off) if g == 0 else (False, False, f"xc{g}", off))
        elif col < OFF_KVB:
            plan.append((True, False, "main", M_KVA + col - OFF_KVA))
        elif col < OFF_KVC:
            plan.append((True, True, "main", M_KVB + col - OFF_KVB))
        else:
            g, off = divmod(col - OFF_KVC, KV_C * W_KV)
            plan.append((True, False, "main", M_KVC0 + off) if g == 0 else (True, False, f"xc{g}", W_C + off))
    return plan


def _prep_kernel(x_ref, w_ref, rope_ref, ones_ref, main_ref, xc1_ref, xc2_ref, stage_ref, *, tp, plan):
    lane = lax.broadcasted_iota(jnp.int32, (tp, LANES), 1)
    is_k = lane < HEAD_DIM
    swap_hi = (lane & 16) != 0
    seg_ones = ones_ref[...]
    n_stage = W_XC // LANES
    for cb, (is_kv, rotary, dest, col) in enumerate(plan):
        cols = slice(cb * LANES, (cb + 1) * LANES)
        x = x_ref[0, :, cols].astype(F32)
        ssq = jnp.dot((x * x).astype(BF16), seg_ones, preferred_element_type=F32)
        inv = lax.rsqrt(ssq * (1.0 / HEAD_DIM) + RMS_EPS)
        if is_kv:
            inv = jnp.where(is_k, inv, 1.0)
        y = x * inv * w_ref[:, cols]
        if rotary:
            base = 2 * LANES if is_kv else 0
            cos = rope_ref[:, base:base + LANES]
            sin_signed = rope_ref[:, base + LANES:base + 2 * LANES]
            swapped = jnp.where(swap_hi, pltpu.roll(y, 16, 1), pltpu.roll(y, LANES - 16, 1))
            y = y * cos + swapped * sin_signed
        if dest == "main":
            main_ref[0, :, col:col + LANES] = y.astype(BF16)
        else:
            stage_ref[(int(dest[2]) - 1) * n_stage + col // LANES] = y
    for g, xc_ref in ((1, xc1_ref), (2, xc2_ref)):
        d = C_PATTERNS[g][1]
        for r in range(d):
            for k in range(n_stage):
                piece = stage_ref[(g - 1) * n_stage + k, pl.ds(r, tp // d, stride=d), :]
                xc_ref[0, r, :, k * LANES:(k + 1) * LANES] = piece.astype(BF16)


def _prepare(proj, wvec, rope, seg_ones, *, tp=256):
    B, S, _ = proj.shape
    d1, d2 = C_PATTERNS[1][1], C_PATTERNS[2][1]
    plan = _prep_plan()
    return pl.pallas_call(
        functools.partial(_prep_kernel, tp=tp, plan=plan),
        grid=(B, S // tp),
        in_specs=[pl.BlockSpec((1, tp, W_QKV), lambda b, i: (b, i, 0)),
                  pl.BlockSpec((1, W_QKV), lambda b, i: (0, 0)),
                  pl.BlockSpec((tp, 4 * LANES), lambda b, i: (i, 0)),
                  pl.BlockSpec((LANES, LANES), lambda b, i: (0, 0))],
        out_specs=[pl.BlockSpec((1, tp, W_MAIN), lambda b, i: (b, i, 0)),
                   pl.BlockSpec((1, d1, tp // d1, W_XC), lambda b, i: (b, 0, i, 0)),
                   pl.BlockSpec((1, d2, tp // d2, W_XC), lambda b, i: (b, 0, i, 0))],
        out_shape=[jax.ShapeDtypeStruct((B, S, W_MAIN), BF16),
                   jax.ShapeDtypeStruct((B, d1, S // d1, W_XC), BF16),
                   jax.ShapeDtypeStruct((B, d2, S // d2, W_XC), BF16)],
        scratch_shapes=[pltpu.VMEM((2 * (W_XC // LANES), tp, LANES), F32)],
        compiler_params=pltpu.CompilerParams(dimension_semantics=("parallel", "parallel"),
                                             vmem_limit_bytes=VMEM_LIMIT),
        name="prepare",
    )(proj, wvec, rope, seg_ones)


def _band_kernel(*refs, nsub, blk_axis, with_sink, halo, classes):
    if classes is None:
        _band_rows(*refs, nsub=nsub, blk_axis=blk_axis, with_sink=with_sink, halo=halo)
        return
    q_ref, kp_ref, kc_ref, kn_ref, bias_ref, o_ref, l_ref = refs
    for c in range(classes):
        _band_rows(q_ref.at[c], kp_ref.at[c], kc_ref.at[c], kn_ref.at[c], bias_ref, o_ref.at[c], l_ref.at[c],
                   nsub=nsub, blk_axis=blk_axis, with_sink=with_sink, halo=halo)


def _band_rows(*refs, nsub, blk_axis, with_sink, halo):
    if with_sink:
        q_ref, kp_ref, kc_ref, kn_ref, bias_ref, sink_ref, gate_ref, o_ref = refs
    else:
        q_ref, kp_ref, kc_ref, kn_ref, bias_ref, o_ref, l_ref = refs
    i = pl.program_id(blk_axis)
    nb = pl.num_programs(blk_axis)
    kv = jnp.concatenate([kp_ref[...], kc_ref[...], kn_ref[...]], axis=0)
    bias = bias_ref[0]
    width = BLOCK + 2 * halo
    col = lax.broadcasted_iota(jnp.int32, bias.shape, 1)
    for t in range(nsub):
        rows = slice(t * BLOCK, (t + 1) * BLOCK)
        q4 = _stack_heads(q_ref[rows, :])
        start = (t + 1) * BLOCK - halo
        win = kv[start:start + width]
        s = _nt_dot(q4, win[:, :HEAD_DIM]) + bias
        if t == 0:
            s = jnp.where((col < halo) & (i == 0), NEG_INF, s)
        if t == nsub - 1:
            s = jnp.where((col >= BLOCK + halo) & (i == nb - 1), NEG_INF, s)
        m = jnp.broadcast_to(jnp.max(s, axis=-1, keepdims=True), (GQA * BLOCK, LANES))
        if with_sink:
            sink = sink_ref[0]
            m = jnp.maximum(m, sink)
        e = jnp.exp2(s - jnp.concatenate([m] * (width // LANES), axis=1)).astype(BF16)
        pv = jnp.dot(e, jnp.concatenate([win, jnp.ones((width, LANES), BF16)], axis=1), preferred_element_type=F32)
        denom = pv[:, LANES:]
        if with_sink:
            denom = denom + jnp.exp2(sink - m)
        o = pv[:, :LANES] / denom
        low = lax.broadcasted_iota(jnp.int32, (BLOCK, LANES), 1) < HEAD_DIM

        def pair_heads(x, p):
            even, odd = x[2 * p * BLOCK:(2 * p + 1) * BLOCK], x[(2 * p + 1) * BLOCK:(2 * p + 2) * BLOCK]
            return even, odd

        pairs = []
        for p in range(GQA // 2):
            even, odd = pair_heads(o, p)
            pairs.append(jnp.where(low, pltpu.roll(even, HEAD_DIM, 1), odd))
        o2 = jnp.concatenate(pairs, axis=1)
        if with_sink:
            g = gate_ref[rows, :].astype(F32)
            o_ref[rows, :] = (o2 * _silu(g)).astype(o_ref.dtype)
        else:
            o_ref[rows, :] = o2
            lse = m + jnp.log2(denom)
            l_ref[rows, :] = jnp.concatenate([jnp.where(low, *pair_heads(lse, p)) for p in range(GQA // 2)], axis=1)


def _band_specs(lead_block, lead_of, tq, nblk, q_blk, kv_blk):
    ratio = tq // BLOCK

    def spec(rows, width, row_fn, col_fn):
        return pl.BlockSpec(lead_block + (rows, width), lambda *g: lead_of(*g) + (row_fn(g[-1]), col_fn(*g)))

    return [spec(tq, W_QG, lambda u: u, q_blk),
            spec(BLOCK, W_KV, lambda u: jnp.maximum(u * ratio - 1, 0), kv_blk),
            spec(tq, W_KV, lambda u: u, kv_blk),
            spec(BLOCK, W_KV, lambda u: jnp.minimum((u + 1) * ratio, nblk * ratio - 1), kv_blk)]


def _window_sink_attention(main, proj, bias, sink):
    B, S, _ = main.shape
    tq = min(BAND_ROWS, S)
    nblk = S // tq
    lead_of = lambda b, h, u: (b,)
    in_specs = _band_specs((None,), lead_of, tq, nblk,
                           lambda b, h, u: M_QA // W_QG + h, lambda b, h, u: M_KVA // W_KV + h)
    in_specs += [pl.BlockSpec((1, GQA * BLOCK, BLOCK + 2 * WIN_A), lambda b, h, u: (h, 0, 0)),
                 pl.BlockSpec((1, GQA * BLOCK, LANES), lambda b, h, u: (h, 0, 0)),
                 pl.BlockSpec((None, tq, W_QG), lambda b, h, u: (b, u, OFF_GA // W_QG + h))]
    return pl.pallas_call(
        functools.partial(_band_kernel, nsub=tq // BLOCK, blk_axis=2, with_sink=True, halo=WIN_A, classes=None),
        grid=(B, KV_A, nblk),
        in_specs=in_specs,
        out_specs=pl.BlockSpec((None, tq, W_QG), lambda b, h, u: (b, u, h)),
        out_shape=jax.ShapeDtypeStruct((B, S, H_A * HEAD_DIM), BF16),
        compiler_params=pltpu.CompilerParams(dimension_semantics=("parallel",) * 3, vmem_limit_bytes=VMEM_LIMIT),
        name="mixer_a",
    )(main, main, main, main, bias, sink, proj)


def _dilated_group_attention(x, bias, group):
    if group == 0:
        B, sub, _ = x.shape
        tq = min(BAND_ROWS, sub)
        d, classes, lead_block = 1, None, (None,)
        lead_of = lambda b, r, j, u: (b,)
        q0, kv0 = M_QC0 // W_QG, M_KVC0 // W_KV
        out_shape = jax.ShapeDtypeStruct((B, sub, W_C), F32)
    else:
        B, d, sub, _ = x.shape
        tq = min(BAND_ROWS, sub)
        classes = min(d, BAND_ROWS // tq)
        lead_block = (None, classes)
        lead_of = lambda b, r, j, u: (b, r)
        q0, kv0 = 0, W_C // W_KV
        out_shape = jax.ShapeDtypeStruct((B, d, sub, W_C), F32)
    nblk = sub // tq
    in_specs = _band_specs(lead_block, lead_of, tq, nblk, lambda b, r, j, u: q0 + j, lambda b, r, j, u: kv0 + j)
    in_specs += [pl.BlockSpec((1, GQA * BLOCK, BLOCK + 2 * C_RADIUS), lambda b, r, j, u: (j, 0, 0))]
    out_spec = pl.BlockSpec(lead_block + (tq, W_QG), lambda b, r, j, u: lead_of(b, r, j, u) + (u, j))
    return pl.pallas_call(
        functools.partial(_band_kernel, nsub=tq // BLOCK, blk_axis=3, with_sink=False, halo=C_RADIUS,
                          classes=classes),
        grid=(B, d // (classes or 1), KV_C, nblk),
        in_specs=in_specs,
        out_specs=[out_spec, out_spec],
        out_shape=[out_shape, out_shape],
        compiler_params=pltpu.CompilerParams(dimension_semantics=("parallel",) * 4, vmem_limit_bytes=VMEM_LIMIT),
        name=f"mixer_c{group}",
    )(x, x, x, x, bias)


def _merge_kernel(o0_ref, l0_ref, o1_ref, l1_ref, o2_ref, l2_ref, gate_ref, out_ref, stage_ref, *, tm):
    nk = W_C // LANES
    for a, (ref, g) in enumerate(((o1_ref, 1), (l1_ref, 1), (o2_ref, 2), (l2_ref, 2))):
        d = C_PATTERNS[g][1]
        for r in range(d):
            for k in range(nk):
                stage_ref[a * nk + k, pl.ds(r, tm // d, stride=d), :] = ref[0, r, :, k * LANES:(k + 1) * LANES]
    for k in range(nk):
        cols = slice(k * LANES, (k + 1) * LANES)
        o0, l0 = o0_ref[0, :, cols], l0_ref[0, :, cols]
        o1, l1, o2, l2 = stage_ref[k], stage_ref[nk + k], stage_ref[2 * nk + k], stage_ref[3 * nk + k]
        m = jnp.maximum(jnp.maximum(l0, l1), l2)
        w0, w1, w2 = jnp.exp2(l0 - m), jnp.exp2(l1 - m), jnp.exp2(l2 - m)
        y = (w0 * o0 + w1 * o1 + w2 * o2) / (w0 + w1 + w2)
        out_ref[0, :, cols] = (y * _silu(gate_ref[0, :, cols].astype(F32))).astype(out_ref.dtype)


def _merge_groups(outs, lses, proj, *, tm=1024):
    B, S, _ = proj.shape
    d1, d2 = C_PATTERNS[1][1], C_PATTERNS[2][1]
    tok = pl.BlockSpec((1, tm, W_C), lambda b, i: (b, i, 0))
    res1 = pl.BlockSpec((1, d1, tm // d1, W_C), lambda b, i: (b, 0, i, 0))
    res2 = pl.BlockSpec((1, d2, tm // d2, W_C), lambda b, i: (b, 0, i, 0))
    return pl.pallas_call(
        functools.partial(_merge_kernel, tm=tm),
        grid=(B, S // tm),
        in_specs=[tok, tok, res1, res1, res2, res2, pl.BlockSpec((1, tm, W_C), lambda b, i: (b, i, OFF_GC // W_C))],
        out_specs=tok,
        out_shape=jax.ShapeDtypeStruct((B, S, W_C), BF16),
        scratch_shapes=[pltpu.VMEM((4 * (W_C // LANES), tm, LANES), F32)],
        compiler_params=pltpu.CompilerParams(dimension_semantics=("parallel", "parallel"),
                                             vmem_limit_bytes=VMEM_LIMIT),
        name="mixer_c_merge",
    )(outs[0], lses[0], outs[1], lses[1], outs[2], lses[2], proj)


def _global_kernel(q_ref, kv_ref, gate_ref, o_ref, v_s, *, seq, tk, prep_rows):
    @pl.when(pl.program_id(2) == 0)
    def _stage_values():
        def body(c, carry):
            rows = pl.ds(pl.multiple_of(c * prep_rows, prep_rows), prep_rows)
            v_s[rows, :] = jnp.concatenate([kv_ref[rows, HEAD_DIM:], jnp.ones((prep_rows, HEAD_DIM), BF16)], axis=1)
            return carry

        lax.fori_loop(0, seq // prep_rows, body, 0)

    q4 = _stack_heads(q_ref[...])
    rows4 = q4.shape[0]
    m = jnp.full((rows4, 1), NEG_INF, F32)
    acc = jnp.zeros((rows4, 2 * HEAD_DIM), F32)
    for c in range(seq // tk):
        rows = slice(c * tk, (c + 1) * tk)
        s = _nt_dot(q4, kv_ref[rows, :HEAD_DIM])
        m_new = jnp.maximum(m, jnp.max(s, axis=-1, keepdims=True))
        p = jnp.exp2(s - m_new).astype(BF16)
        acc = jnp.exp2(m - m_new) * acc + jnp.dot(p, v_s[rows, :], preferred_element_type=F32)
        m = m_new
    o = acc[:, :HEAD_DIM] / acc[:, HEAD_DIM:HEAD_DIM + 1]
    g = gate_ref[...].astype(F32)
    o_ref[...] = (_unstack_heads(o) * _silu(g)).astype(o_ref.dtype)


def _global_axial_attention(main, proj, *, tq=512, tk=512):
    B, S, _ = main.shape
    return pl.pallas_call(
        functools.partial(_global_kernel, seq=S, tk=tk, prep_rows=512),
        grid=(B, KV_B, S // tq),
        in_specs=[pl.BlockSpec((None, tq, W_QG), lambda b, h, i: (b, i, M_QB // W_QG + h)),
                  pl.BlockSpec((None, S, W_KV), lambda b, h, i: (b, 0, M_KVB // W_KV + h)),
                  pl.BlockSpec((None, tq, W_QG), lambda b, h, i: (b, i, OFF_GB // W_QG + h))],
        out_specs=pl.BlockSpec((None, tq, W_QG), lambda b, h, i: (b, i, h)),
        out_shape=jax.ShapeDtypeStruct((B, S, H_B * HEAD_DIM), BF16),
        scratch_shapes=[pltpu.VMEM((S, 2 * HEAD_DIM), BF16)],
        compiler_params=pltpu.CompilerParams(dimension_semantics=("parallel", "parallel", "arbitrary"),
                                             vmem_limit_bytes=VMEM_LIMIT),
        name="mixer_b",
    )(main, main, proj)


def _outproj_kernel(x_ref, a_ref, b_ref, c_ref, wa_ref, wb_ref, wc_ref, o_ref):
    y = jnp.dot(a_ref[...], wa_ref[...], preferred_element_type=F32)
    y += jnp.dot(b_ref[...], wb_ref[...], preferred_element_type=F32)
    y += jnp.dot(c_ref[...], wc_ref[...], preferred_element_type=F32)
    o_ref[...] = x_ref[...] + y


def _outproj(x2, mix_a, mix_b, mix_c, w, *, tm=512, tn=D_MODEL):
    T = x2.shape[0]
    ka, kb, kc = H_A * HEAD_DIM, H_B * HEAD_DIM, W_C
    return pl.pallas_call(
        _outproj_kernel,
        grid=(T // tm, D_MODEL // tn),
        in_specs=[pl.BlockSpec((tm, tn), lambda i, j: (i, j)),
                  pl.BlockSpec((tm, ka), lambda i, j: (i, 0)),
                  pl.BlockSpec((tm, kb), lambda i, j: (i, 0)),
                  pl.BlockSpec((tm, kc), lambda i, j: (i, 0)),
                  pl.BlockSpec((ka, tn), lambda i, j: (0, j)),
                  pl.BlockSpec((kb, tn), lambda i, j: (1, j)),
                  pl.BlockSpec((kc, tn), lambda i, j: ((ka + kb) // kc, j))],
        out_specs=pl.BlockSpec((tm, tn), lambda i, j: (i, j)),
        out_shape=jax.ShapeDtypeStruct((T, D_MODEL), F32),
        compiler_params=pltpu.CompilerParams(dimension_semantics=("parallel", "parallel"),
                                             vmem_limit_bytes=VMEM_LIMIT),
        name="outproj",
    )(x2, mix_a, mix_b, mix_c, w, w, w)


def _rope_table(S):
    rows = S // GRID_W
    row = jnp.repeat(jnp.arange(rows, dtype=jnp.int32), GRID_W)
    col = jnp.arange(S, dtype=jnp.int32) % GRID_W
    n_freq = HEAD_DIM // 4
    inv_freq = ROPE_THETA ** (-jnp.arange(n_freq, dtype=F32) / n_freq)
    ang_row = row.astype(F32)[:, None] * inv_freq[None, :]
    ang_col = col.astype(F32)[:, None] * inv_freq[None, :]
    cr, sr, cc, sc = jnp.cos(ang_row), jnp.sin(ang_row), jnp.cos(ang_col), jnp.sin(ang_col)
    cos = jnp.concatenate([cr, cr, cc, cc], axis=1)
    sin_signed = jnp.concatenate([-sr, sr, -sc, sc], axis=1)
    one, zero = jnp.ones_like(cos), jnp.zeros_like(cos)
    return jnp.concatenate([cos, cos, sin_signed, sin_signed, cos, one, sin_signed, zero], axis=1)


def _trunk(x, p):
    B, S, _ = x.shape
    T = B * S
    rope = _rope_table(S)
    x2 = x.reshape(T, D_MODEL)
    for l in range(DEPTH):
        proj = _inproj(x2, p["ln_g"][l], p["w_in"][l]).reshape(B, S, D_IN_PROJ)
        main, xc1, xc2 = _prepare(proj, p["wvec"][l], rope, p["seg_ones"])
        mix_a = _window_sink_attention(main, proj, p["bias_a"], p["sink_a"][l])
        mix_b = _global_axial_attention(main, proj)
        outs, lses = zip(*[_dilated_group_attention(xg, p["bias_c"][g], g) for g, xg in enumerate((main, xc1, xc2))])
        mix_c = _merge_groups(outs, lses, proj)
        x2 = _outproj(x2, mix_a.reshape(T, -1), mix_b.reshape(T, -1), mix_c.reshape(T, -1), p["w_out"][l])
    return x2.reshape(B, S, D_MODEL)


def _prepare_params(ln_g, w_in, q_norm_a, k_norm_a, sink_a, q_norm_b, k_norm_b, q_norm_c, k_norm_c, rel_bias, w_out):
    heads_c = [list(range(H_A + g * H_C, H_A + (g + 1) * H_C)) for g in range(N_C_GROUPS)]
    one = jnp.ones((DEPTH, HEAD_DIM), F32)

    def q_cols(w, n):
        return jnp.tile(w.astype(F32) * Q_SCALE, (1, n))

    def kv_cols(w, n):
        return jnp.tile(jnp.concatenate([w.astype(F32), one], axis=1), (1, n))

    wvec = jnp.concatenate([q_cols(q_norm_a, H_A), q_cols(q_norm_b, H_B), q_cols(q_norm_c, N_C_GROUPS * H_C),
                            kv_cols(k_norm_a, KV_A), kv_cols(k_norm_b, KV_B), kv_cols(k_norm_c, N_C_GROUPS * KV_C)],
                           axis=1)
    seg = np.arange(LANES) // HEAD_DIM
    return {
        "ln_g": ln_g.reshape(DEPTH, 1, D_MODEL),
        "w_in": _permute_columns(w_in, _proj_column_order()).astype(BF16),
        "w_out": w_out.astype(BF16),
        "wvec": wvec.reshape(DEPTH, 1, W_QKV),
        "seg_ones": jnp.asarray(seg[:, None] == seg[None, :], BF16),
        "sink_a": jnp.broadcast_to(jnp.repeat(sink_a.astype(F32) * LOG2E, BLOCK, axis=1)[:, :, None],
                                   (DEPTH, H_A * BLOCK, LANES)).reshape(DEPTH, KV_A, GQA * BLOCK, LANES),
        "bias_a": _band_bias(rel_bias, list(range(H_A)), 1, WIN_A, WIN_A),
        "bias_c": [_band_bias(rel_bias, heads_c[g], d, C_RADIUS, C_RADIUS) for g, (w, d) in enumerate(C_PATTERNS)],
    }


def kernel(x_prompt, x_sample, ln_g, w_in, q_norm_a, k_norm_a, sink_a, q_norm_b, k_norm_b,
           q_norm_c, k_norm_c, rel_bias, w_out):
    p = _prepare_params(ln_g, w_in, q_norm_a, k_norm_a, sink_a, q_norm_b, k_norm_b, q_norm_c, k_norm_c,
                        rel_bias, w_out)
    return _trunk(x_prompt, p), _trunk(x_sample, p)
```

```python
import functools
import math

import numpy as np
import jax
import jax.numpy as jnp
from jax import lax
from jax.experimental import pallas as pl
from jax.experimental.pallas import tpu as pltpu

F32 = jnp.float32
BF16 = jnp.bfloat16

D_MODEL = 2048
DEPTH = 2
HEAD_DIM = 64
N_HEADS_TOTAL = D_MODEL // HEAD_DIM
H_C = N_HEADS_TOTAL // 4
H_A = (N_HEADS_TOTAL - H_C) // 2
H_B = N_HEADS_TOTAL - H_C - H_A
KV_A = H_A // 4
KV_B = H_B // 4
KV_C = H_C // 4
GQA = 4
C_PATTERNS = ((128, 1), (512, 4), (2048, 16))
N_C_GROUPS = len(C_PATTERNS)
C_RADIUS = 64
assert all(w // (2 * d) == C_RADIUS for w, d in C_PATTERNS)
BLOCK = 128
LANES = 128
WIN_A = 128
GRID_W = 64
ROPE_THETA = 10000.0
T5_BUCKETS = 32
T5_MAX_DISTANCE = 1024
RMS_EPS = 1e-6
NEG_INF = -1e30
LOG2E = math.log2(math.e)
Q_SCALE = HEAD_DIM ** -0.5 * LOG2E

W_QG = GQA * HEAD_DIM
W_KV = 2 * HEAD_DIM
W_C = H_C * HEAD_DIM

OFF_QA = 0
OFF_QB = OFF_QA + H_A * HEAD_DIM
OFF_QC = OFF_QB + H_B * HEAD_DIM
OFF_KVA = OFF_QC + N_C_GROUPS * W_C
OFF_KVB = OFF_KVA + KV_A * W_KV
OFF_KVC = OFF_KVB + KV_B * W_KV
W_QKV = OFF_KVC + N_C_GROUPS * KV_C * W_KV
OFF_GA = W_QKV
OFF_GB = OFF_GA + H_A * HEAD_DIM
OFF_GC = OFF_GB + H_B * HEAD_DIM
D_IN_PROJ = OFF_GC + W_C

M_QA = 0
M_QB = M_QA + H_A * HEAD_DIM
M_QC0 = M_QB + H_B * HEAD_DIM
M_KVA = M_QC0 + W_C
M_KVB = M_KVA + KV_A * W_KV
M_KVC0 = M_KVB + KV_B * W_KV
W_MAIN = M_KVC0 + KV_C * W_KV
W_XC = W_C + KV_C * W_KV

VMEM_LIMIT = 48 * 1024 * 1024
VMEM_LIMIT_OUTPROJ = 54 * 1024 * 1024
OUTPROJ_PIECES = 2
BAND_ROWS = 1024


def _proj_column_order():
    sizes = (H_A * 64, KV_A * 64, KV_A * 64, H_A * 64, H_B * 64, KV_B * 64, KV_B * 64, H_B * 64,
             N_C_GROUPS * H_C * 64, N_C_GROUPS * KV_C * 64, N_C_GROUPS * KV_C * 64, H_C * 64)
    o = np.concatenate([[0], np.cumsum(sizes)])
    qa, ka, va, ga, qb, kb, vb, gb, qc, kc, vc, gc = (np.arange(o[i], o[i + 1]) for i in range(12))

    def kv_pairs(k, v, n):
        return np.concatenate([np.concatenate([k[h * 64:(h + 1) * 64], v[h * 64:(h + 1) * 64]]) for h in range(n)])

    order = np.concatenate([qa, qb, qc, kv_pairs(ka, va, KV_A), kv_pairs(kb, vb, KV_B),
                            kv_pairs(kc, vc, N_C_GROUPS * KV_C), ga, gb, gc])
    assert order.shape == (D_IN_PROJ,) and np.array_equal(np.sort(order), np.arange(D_IN_PROJ))
    return order


def _permute_columns(w, order):
    cuts = np.flatnonzero(np.diff(order) != 1) + 1
    starts = np.concatenate([[0], cuts])
    stops = np.concatenate([cuts, [order.size]])
    return jnp.concatenate([w[..., int(order[a]):int(order[b - 1]) + 1] for a, b in zip(starts, stops)], axis=-1)


def _t5_bucket(rel):
    half = T5_BUCKETS // 2
    max_exact = half // 2
    n = np.abs(rel)
    log_ratio = np.log(np.maximum(n, 1).astype(np.float64) / max_exact) / math.log(T5_MAX_DISTANCE / max_exact)
    large = np.minimum(max_exact + (log_ratio * (half - max_exact)).astype(np.int32), half - 1)
    return np.where(rel > 0, half, 0) + np.where(n < max_exact, n, large)


def _band_bias(rel_bias, heads, dilation, radius, halo):
    width = BLOCK + 2 * halo
    n = width + BLOCK - 1
    rel = np.arange(n) - (halo + BLOCK - 1)
    tab = rel_bias[_t5_bucket(rel * dilation)][:, heads[0]:heads[0] + len(heads)].astype(F32) * LOG2E
    tab = jnp.where((np.abs(rel) <= radius)[:, None], tab, NEG_INF)
    stream = jnp.tile(jnp.pad(tab, ((0, 1), (0, 0))), (BLOCK, 1))[:BLOCK * n]
    b = stream.reshape(BLOCK, n, len(heads))[:, BLOCK - 1:, :]
    return b.transpose(2, 0, 1).reshape(len(heads) // GQA, GQA * BLOCK, width)


def _rms(x, w):
    return x * lax.rsqrt(jnp.mean(x * x, axis=-1, keepdims=True) + RMS_EPS) * w


def _stack_heads(x):
    return jnp.concatenate([x[:, HEAD_DIM * r:HEAD_DIM * (r + 1)] for r in range(GQA)], axis=0)


def _unstack_heads(x):
    rows = x.shape[0] // GQA
    return jnp.concatenate([x[rows * r:rows * (r + 1)] for r in range(GQA)], axis=1)


def _silu(g):
    return g * jax.nn.sigmoid(g)


def _nt_dot(a, b):
    return lax.dot_general(a, b, (((1,), (1,)), ((), ())), preferred_element_type=F32)


def _inproj_kernel(x_ref, g_ref, w_ref, o_ref, h_ref):
    @pl.when(pl.program_id(1) == 0)
    def _():
        x = x_ref[...]
        h_ref[...] = _rms(x, g_ref[...]).astype(BF16)

    o_ref[...] = jnp.dot(h_ref[...], w_ref[...], preferred_element_type=F32).astype(o_ref.dtype)


def _inproj(x2, g, w, *, tm=1024, tn=D_IN_PROJ // 4):
    T = x2.shape[0]
    return pl.pallas_call(
        _inproj_kernel,
        grid=(T // tm, D_IN_PROJ // tn),
        in_specs=[pl.BlockSpec((tm, D_MODEL), lambda i, j: (i, 0)),
                  pl.BlockSpec((1, D_MODEL), lambda i, j: (0, 0)),
                  pl.BlockSpec((D_MODEL, tn), lambda i, j: (0, j))],
        out_specs=pl.BlockSpec((tm, tn), lambda i, j: (i, j)),
        out_shape=jax.ShapeDtypeStruct((T, D_IN_PROJ), BF16),
        scratch_shapes=[pltpu.VMEM((tm, D_MODEL), BF16)],
        compiler_params=pltpu.CompilerParams(dimension_semantics=("parallel", "arbitrary"),
                                             vmem_limit_bytes=VMEM_LIMIT),
        name="inproj",
    )(x2, g, w)


def _prep_plan():
    plan = []
    for cb in range(W_QKV // LANES):
        col = cb * LANES
        if col < OFF_QB:
            plan.append((False, False, "main", M_QA + col - OFF_QA))
        elif col < OFF_QC:
            plan.append((False, True, "main", M_QB + col - OFF_QB))
        elif col < OFF_KVA:
            g, off = divmod(col - OFF_QC, W_C)
            plan.append((False, False, "main", M_QC0 + off) if g == 0 else (False, False, f"xc{g}", off))
        elif col < OFF_KVB:
            plan.append((True, False, "main", M_KVA + col - OFF_KVA))
        elif col < OFF_KVC:
            plan.append((True, True, "main", M_KVB + col - OFF_KVB))
        else:
            g, off = divmod(col - OFF_KVC, KV_C * W_KV)
            plan.append((True, False, "main", M_KVC0 + off) if g == 0 else (True, False, f"xc{g}", W_C + off))
    return plan


def _prep_kernel(x_ref, w_ref, rope_ref, ones_ref, main_ref, xc1_ref, xc2_ref, stage_ref, *, tp, plan):
    lane = lax.broadcasted_iota(jnp.int32, (tp, LANES), 1)
    is_k = lane < HEAD_DIM
    swap_hi = (lane & 16) != 0
    seg_ones = ones_ref[...]
    n_stage = W_XC // LANES
    for cb, (is_kv, rotary, dest, col) in enumerate(plan):
        cols = slice(cb * LANES, (cb + 1) * LANES)
        x = x_ref[0, :, cols].astype(F32)
        ssq = jnp.dot((x * x).astype(BF16), seg_ones, preferred_element_type=F32)
        inv = lax.rsqrt(ssq * (1.0 / HEAD_DIM) + RMS_EPS)
        if is_kv:
            inv = jnp.where(is_k, inv, 1.0)
        y = x * inv * w_ref[:, cols]
        if rotary:
            base = 2 * LANES if is_kv else 0
            cos = rope_ref[:, base:base + LANES]
            sin_signed = rope_ref[:, base + LANES:base + 2 * LANES]
            swapped = jnp.where(swap_hi, pltpu.roll(y, 16, 1), pltpu.roll(y, LANES - 16, 1))
            y = y * cos + swapped * sin_signed
        if dest == "main":
            main_ref[0, :, col:col + LANES] = y.astype(BF16)
        else:
            stage_ref[(int(dest[2]) - 1) * n_stage + col // LANES] = y
    for g, xc_ref in ((1, xc1_ref), (2, xc2_ref)):
        d = C_PATTERNS[g][1]
        for r in range(d):
            for k in range(n_stage):
                piece = stage_ref[(g - 1) * n_stage + k, pl.ds(r, tp // d, stride=d), :]
                xc_ref[0, r, :, k * LANES:(k + 1) * LANES] = piece.astype(BF16)


def _prepare(proj, wvec, rope, seg_ones, *, tp=256):
    B, S, _ = proj.shape
    d1, d2 = C_PATTERNS[1][1], C_PATTERNS[2][1]
    plan = _prep_plan()
    return pl.pallas_call(
        functools.partial(_prep_kernel, tp=tp, plan=plan),
        grid=(B, S // tp),
        in_specs=[pl.BlockSpec((1, tp, W_QKV), lambda b, i: (b, i, 0)),
                  pl.BlockSpec((1, W_QKV), lambda b, i: (0, 0)),
                  pl.BlockSpec((tp, 4 * LANES), lambda b, i: (i, 0)),
                  pl.BlockSpec((LANES, LANES), lambda b, i: (0, 0))],
        out_specs=[pl.BlockSpec((1, tp, W_MAIN), lambda b, i: (b, i, 0)),
                   pl.BlockSpec((1, d1, tp // d1, W_XC), lambda b, i: (b, 0, i, 0)),
                   pl.BlockSpec((1, d2, tp // d2, W_XC), lambda b, i: (b, 0, i, 0))],
        out_shape=[jax.ShapeDtypeStruct((B, S, W_MAIN), BF16),
                   jax.ShapeDtypeStruct((B, d1, S // d1, W_XC), BF16),
                   jax.ShapeDtypeStruct((B, d2, S // d2, W_XC), BF16)],
        scratch_shapes=[pltpu.VMEM((2 * (W_XC // LANES), tp, LANES), F32)],
        compiler_params=pltpu.CompilerParams(dimension_semantics=("parallel", "parallel"),
                                             vmem_limit_bytes=VMEM_LIMIT),
        name="prepare",
    )(proj, wvec, rope, seg_ones)


def _band_kernel(*refs, nsub, blk_axis, with_sink, halo, classes):
    if classes is None:
        _band_rows(*refs, nsub=nsub, blk_axis=blk_axis, with_sink=with_sink, halo=halo)
        return
    q_ref, kp_ref, kc_ref, kn_ref, bias_ref, o_ref, l_ref = refs
    for c in range(classes):
        _band_rows(q_ref.at[c], kp_ref.at[c], kc_ref.at[c], kn_ref.at[c], bias_ref, o_ref.at[c], l_ref.at[c],
                   nsub=nsub, blk_axis=blk_axis, with_sink=with_sink, halo=halo)


def _band_rows(*refs, nsub, blk_axis, with_sink, halo):
    if with_sink:
        q_ref, kp_ref, kc_ref, kn_ref, bias_ref, sink_ref, gate_ref, o_ref = refs
    else:
        q_ref, kp_ref, kc_ref, kn_ref, bias_ref, o_ref, l_ref = refs
    i = pl.program_id(blk_axis)
    nb = pl.num_programs(blk_axis)
    kv = jnp.concatenate([kp_ref[...], kc_ref[...], kn_ref[...]], axis=0)
    bias = bias_ref[0]
    width = BLOCK + 2 * halo
    col = lax.broadcasted_iota(jnp.int32, bias.shape, 1)
    for t in range(nsub):
        rows = slice(t * BLOCK, (t + 1) * BLOCK)
        q4 = _stack_heads(q_ref[rows, :])
        start = (t + 1) * BLOCK - halo
        win = kv[start:start + width]
        s = _nt_dot(q4, win[:, :HEAD_DIM]) + bias
        if t == 0:
            s = jnp.where((col < halo) & (i == 0), NEG_INF, s)
        if t == nsub - 1:
            s = jnp.where((col >= BLOCK + halo) & (i == nb - 1), NEG_INF, s)
        m = jnp.broadcast_to(jnp.max(s, axis=-1, keepdims=True), (GQA * BLOCK, LANES))
        if with_sink:
            sink = sink_ref[0]
            m = jnp.maximum(m, sink)
        e = jnp.exp2(s - jnp.concatenate([m] * (width // LANES), axis=1)).astype(BF16)
        pv = jnp.dot(e, jnp.concatenate([win, jnp.ones((width, LANES), BF16)], axis=1), preferred_element_type=F32)
        denom = pv[:, LANES:]
        if with_sink:
            denom = denom + jnp.exp2(sink - m)
        o = pv[:, :LANES] / denom
        low = lax.broadcasted_iota(jnp.int32, (BLOCK, LANES), 1) < HEAD_DIM

        def pair_heads(x, p):
            even, odd = x[2 * p * BLOCK:(2 * p + 1) * BLOCK], x[(2 * p + 1) * BLOCK:(2 * p + 2) * BLOCK]
            return even, odd

        pairs = []
        for p in range(GQA // 2):
            even, odd = pair_heads(o, p)
            pairs.append(jnp.where(low, pltpu.roll(even, HEAD_DIM, 1), odd))
        o2 = jnp.concatenate(pairs, axis=1)
        if with_sink:
            g = gate_ref[rows, :].astype(F32)
            o_ref[rows, :] = (o2 * _silu(g)).astype(o_ref.dtype)
        else:
            o_ref[rows, :] = o2
            lse = m + jnp.log2(denom)
            l_ref[rows, :] = jnp.concatenate([jnp.where(low, *pair_heads(lse, p)) for p in range(GQA // 2)], axis=1)


def _band_specs(lead_block, lead_of, tq, nblk, q_blk, kv_blk):
    ratio = tq // BLOCK

    def spec(rows, width, row_fn, col_fn):
        return pl.BlockSpec(lead_block + (rows, width), lambda *g: lead_of(*g) + (row_fn(g[-1]), col_fn(*g)))

    return [spec(tq, W_QG, lambda u: u, q_blk),
            spec(BLOCK, W_KV, lambda u: jnp.maximum(u * ratio - 1, 0), kv_blk),
            spec(tq, W_KV, lambda u: u, kv_blk),
            spec(BLOCK, W_KV, lambda u: jnp.minimum((u + 1) * ratio, nblk * ratio - 1), kv_blk)]


def _window_sink_attention(main, proj, bias, sink):
    B, S, _ = main.shape
    tq = min(BAND_ROWS, S)
    nblk = S // tq
    lead_of = lambda b, h, u: (b,)
    in_specs = _band_specs((None,), lead_of, tq, nblk,
                           lambda b, h, u: M_QA // W_QG + h, lambda b, h, u: M_KVA // W_KV + h)
    in_specs += [pl.BlockSpec((1, GQA * BLOCK, BLOCK + 2 * WIN_A), lambda b, h, u: (h, 0, 0)),
                 pl.BlockSpec((1, GQA * BLOCK, LANES), lambda b, h, u: (h, 0, 0)),
                 pl.BlockSpec((None, tq, W_QG), lambda b, h, u: (b, u, OFF_GA // W_QG + h))]
    return pl.pallas_call(
        functools.partial(_band_kernel, nsub=tq // BLOCK, blk_axis=2, with_sink=True, halo=WIN_A, classes=None),
        grid=(B, KV_A, nblk),
        in_specs=in_specs,
        out_specs=pl.BlockSpec((None, tq, W_QG), lambda b, h, u: (b, u, h)),
        out_shape=jax.ShapeDtypeStruct((B, S, H_A * HEAD_DIM), BF16),
        compiler_params=pltpu.CompilerParams(dimension_semantics=("parallel",) * 3, vmem_limit_bytes=VMEM_LIMIT),
        name="mixer_a",
    )(main, main, main, main, bias, sink, proj)


def _dilated_group_attention(x, bias, group):
    if group == 0:
        B, sub, _ = x.shape
        tq = min(BAND_ROWS, sub)
        d, classes, lead_block = 1, None, (None,)
        lead_of = lambda b, r, j, u: (b,)
        q0, kv0 = M_QC0 // W_QG, M_KVC0 // W_KV
        out_shape = jax.ShapeDtypeStruct((B, sub, W_C), F32)
    else:
        B, d, sub, _ = x.shape
        tq = min(BAND_ROWS, sub)
        classes = min(d, BAND_ROWS // tq)
        lead_block = (None, classes)
        lead_of = lambda b, r, j, u: (b, r)
        q0, kv0 = 0, W_C // W_KV
        out_shape = jax.ShapeDtypeStruct((B, d, sub, W_C), F32)
    nblk = sub // tq
    in_specs = _band_specs(lead_block, lead_of, tq, nblk, lambda b, r, j, u: q0 + j, lambda b, r, j, u: kv0 + j)
    in_specs += [pl.BlockSpec((1, GQA * BLOCK, BLOCK + 2 * C_RADIUS), lambda b, r, j, u: (j, 0, 0))]
    out_spec = pl.BlockSpec(lead_block + (tq, W_QG), lambda b, r, j, u: lead_of(b, r, j, u) + (u, j))
    return pl.pallas_call(
        functools.partial(_band_kernel, nsub=tq // BLOCK, blk_axis=3, with_sink=False, halo=C_RADIUS,
                          classes=classes),
        grid=(B, d // (classes or 1), KV_C, nblk),
        in_specs=in_specs,
        out_specs=[out_spec, out_spec],
        out_shape=[out_shape, out_shape],
        compiler_params=pltpu.CompilerParams(dimension_semantics=("parallel",) * 4, vmem_limit_bytes=VMEM_LIMIT),
        name=f"mixer_c{group}",
    )(x, x, x, x, bias)


def _stage_residue_classes(o1_ref, l1_ref, o2_ref, l2_ref, stage_ref, tm):
    nk = W_C // LANES
    for a, (ref, g) in enumerate(((o1_ref, 1), (l1_ref, 1), (o2_ref, 2), (l2_ref, 2))):
        d = C_PATTERNS[g][1]
        for r in range(d):
            for k in range(nk):
                stage_ref[a * nk + k, pl.ds(r, tm // d, stride=d), :] = ref[0, r, :, k * LANES:(k + 1) * LANES]


def _merge_groups_tile(o0_ref, l0_ref, gate_ref, stage_ref, rows):
    nk = W_C // LANES
    tiles = []
    for k in range(nk):
        cols = slice(k * LANES, (k + 1) * LANES)
        o0, l0 = o0_ref[0, rows, cols], l0_ref[0, rows, cols]
        o1, l1 = stage_ref[k, rows, :], stage_ref[nk + k, rows, :]
        o2, l2 = stage_ref[2 * nk + k, rows, :], stage_ref[3 * nk + k, rows, :]
        m = jnp.maximum(jnp.maximum(l0, l1), l2)
        w0, w1, w2 = jnp.exp2(l0 - m), jnp.exp2(l1 - m), jnp.exp2(l2 - m)
        y = (w0 * o0 + w1 * o1 + w2 * o2) / (w0 + w1 + w2)
        tiles.append((y * _silu(gate_ref[0, rows, cols].astype(F32))).astype(BF16))
    return jnp.concatenate(tiles, axis=1)


def _global_kernel(q_ref, kv_ref, gate_ref, o_ref, v_s, *, seq, tk, prep_rows):
    @pl.when(pl.program_id(2) == 0)
    def _stage_values():
        def body(c, carry):
            rows = pl.ds(pl.multiple_of(c * prep_rows, prep_rows), prep_rows)
            v_s[rows, :] = jnp.concatenate([kv_ref[rows, HEAD_DIM:], jnp.ones((prep_rows, HEAD_DIM), BF16)], axis=1)
            return carry

        lax.fori_loop(0, seq // prep_rows, body, 0)

    q4 = _stack_heads(q_ref[...])
    rows4 = q4.shape[0]
    m = jnp.full((rows4, 1), NEG_INF, F32)
    acc = jnp.zeros((rows4, 2 * HEAD_DIM), F32)
    for c in range(seq // tk):
        rows = slice(c * tk, (c + 1) * tk)
        s = _nt_dot(q4, kv_ref[rows, :HEAD_DIM])
        m_new = jnp.maximum(m, jnp.max(s, axis=-1, keepdims=True))
        p = jnp.exp2(s - m_new).astype(BF16)
        acc = jnp.exp2(m - m_new) * acc + jnp.dot(p, v_s[rows, :], preferred_element_type=F32)
        m = m_new
    o = acc[:, :HEAD_DIM] / acc[:, HEAD_DIM:HEAD_DIM + 1]
    g = gate_ref[...].astype(F32)
    o_ref[...] = (_unstack_heads(o) * _silu(g)).astype(o_ref.dtype)


def _global_axial_attention(main, proj, *, tq=512, tk=512):
    B, S, _ = main.shape
    return pl.pallas_call(
        functools.partial(_global_kernel, seq=S, tk=tk, prep_rows=512),
        grid=(B, KV_B, S // tq),
        in_specs=[pl.BlockSpec((None, tq, W_QG), lambda b, h, i: (b, i, M_QB // W_QG + h)),
                  pl.BlockSpec((None, S, W_KV), lambda b, h, i: (b, 0, M_KVB // W_KV + h)),
                  pl.BlockSpec((None, tq, W_QG), lambda b, h, i: (b, i, OFF_GB // W_QG + h))],
        out_specs=pl.BlockSpec((None, tq, W_QG), lambda b, h, i: (b, i, h)),
        out_shape=jax.ShapeDtypeStruct((B, S, H_B * HEAD_DIM), BF16),
        scratch_shapes=[pltpu.VMEM((S, 2 * HEAD_DIM), BF16)],
        compiler_params=pltpu.CompilerParams(dimension_semantics=("parallel", "parallel", "arbitrary"),
                                             vmem_limit_bytes=VMEM_LIMIT),
        name="mixer_b",
    )(main, main, proj)


def _outproj_kernel(x_ref, a_ref, b_ref, o0_ref, l0_ref, o1_ref, l1_ref, o2_ref, l2_ref, gate_ref,
                    wa_ref, wb_ref, wc_ref, o_ref, stage_ref, *, tm):
    _stage_residue_classes(o1_ref, l1_ref, o2_ref, l2_ref, stage_ref, tm)
    piece = tm // OUTPROJ_PIECES
    for h in range(OUTPROJ_PIECES):
        rows = slice(h * piece, (h + 1) * piece)
        y = jnp.dot(a_ref[0, rows, :], wa_ref[...], preferred_element_type=F32)
        y += jnp.dot(b_ref[0, rows, :], wb_ref[...], preferred_element_type=F32)
        mix_c = _merge_groups_tile(o0_ref, l0_ref, gate_ref, stage_ref, rows)
        y += jnp.dot(mix_c, wc_ref[...], preferred_element_type=F32)
        o_ref[0, rows, :] = x_ref[0, rows, :] + y


def _outproj(x, mix_a, mix_b, outs, lses, proj, w, *, tm=512):
    B, S, _ = x.shape
    ka, kb, kc = H_A * HEAD_DIM, H_B * HEAD_DIM, W_C
    d1, d2 = C_PATTERNS[1][1], C_PATTERNS[2][1]

    def tok(width, col=0):
        return pl.BlockSpec((1, tm, width), lambda b, i: (b, i, col))

    def weight(rows, blk):
        return pl.BlockSpec((rows, D_MODEL), lambda b, i: (blk, 0), pipeline_mode=pl.Buffered(1))

    res1 = pl.BlockSpec((1, d1, tm // d1, kc), lambda b, i: (b, 0, i, 0))
    res2 = pl.BlockSpec((1, d2, tm // d2, kc), lambda b, i: (b, 0, i, 0))
    return pl.pallas_call(
        functools.partial(_outproj_kernel, tm=tm),
        grid=(B, S // tm),
        in_specs=[tok(D_MODEL), tok(ka), tok(kb), tok(kc), tok(kc), res1, res1, res2, res2, tok(kc, OFF_GC // kc),
                  weight(ka, 0), weight(kb, 1), weight(kc, (ka + kb) // kc)],
        out_specs=tok(D_MODEL),
        out_shape=jax.ShapeDtypeStruct((B, S, D_MODEL), F32),
        scratch_shapes=[pltpu.VMEM((4 * (kc // LANES), tm, LANES), F32)],
        compiler_params=pltpu.CompilerParams(dimension_semantics=("parallel", "parallel"),
                                             vmem_limit_bytes=VMEM_LIMIT_OUTPROJ),
        name="outproj",
    )(x, mix_a, mix_b, outs[0], lses[0], outs[1], lses[1], outs[2], lses[2], proj, w, w, w)


def _rope_table(S):
    rows = S // GRID_W
    row = jnp.repeat(jnp.arange(rows, dtype=jnp.int32), GRID_W)
    col = jnp.arange(S, dtype=jnp.int32) % GRID_W
    n_freq = HEAD_DIM // 4
    inv_freq = ROPE_THETA ** (-jnp.arange(n_freq, dtype=F32) / n_freq)
    ang_row = row.astype(F32)[:, None] * inv_freq[None, :]
    ang_col = col.astype(F32)[:, None] * inv_freq[None, :]
    cr, sr, cc, sc = jnp.cos(ang_row), jnp.sin(ang_row), jnp.cos(ang_col), jnp.sin(ang_col)
    cos = jnp.concatenate([cr, cr, cc, cc], axis=1)
    sin_signed = jnp.concatenate([-sr, sr, -sc, sc], axis=1)
    one, zero = jnp.ones_like(cos), jnp.zeros_like(cos)
    return jnp.concatenate([cos, cos, sin_signed, sin_signed, cos, one, sin_signed, zero], axis=1)


def _trunk(x, p):
    B, S, _ = x.shape
    rope = _rope_table(S)
    for l in range(DEPTH):
        proj = _inproj(x.reshape(B * S, D_MODEL), p["ln_g"][l], p["w_in"][l]).reshape(B, S, D_IN_PROJ)
        main, xc1, xc2 = _prepare(proj, p["wvec"][l], rope, p["seg_ones"])
        mix_a = _window_sink_attention(main, proj, p["bias_a"], p["sink_a"][l])
        mix_b = _global_axial_attention(main, proj)
        outs, lses = zip(*[_dilated_group_attention(xg, p["bias_c"][g], g) for g, xg in enumerate((main, xc1, xc2))])
        x = _outproj(x, mix_a, mix_b, outs, lses, proj, p["w_out"][l])
    return x


def _prepare_params(ln_g, w_in, q_norm_a, k_norm_a, sink_a, q_norm_b, k_norm_b, q_norm_c, k_norm_c, rel_bias, w_out):
    heads_c = [list(range(H_A + g * H_C, H_A + (g + 1) * H_C)) for g in range(N_C_GROUPS)]
    one = jnp.ones((DEPTH, HEAD_DIM), F32)

    def q_cols(w, n):
        return jnp.tile(w.astype(F32) * Q_SCALE, (1, n))

    def kv_cols(w, n):
        return jnp.tile(jnp.concatenate([w.astype(F32), one], axis=1), (1, n))

    wvec = jnp.concatenate([q_cols(q_norm_a, H_A), q_cols(q_norm_b, H_B), q_cols(q_norm_c, N_C_GROUPS * H_C),
                            kv_cols(k_norm_a, KV_A), kv_cols(k_norm_b, KV_B), kv_cols(k_norm_c, N_C_GROUPS * KV_C)],
                           axis=1)
    seg = np.arange(LANES) // HEAD_DIM
    return {
        "ln_g": ln_g.reshape(DEPTH, 1, D_MODEL),
        "w_in": _permute_columns(w_in, _proj_column_order()).astype(BF16),
        "w_out": w_out.astype(BF16),
        "wvec": wvec.reshape(DEPTH, 1, W_QKV),
        "seg_ones": jnp.asarray(seg[:, None] == seg[None, :], BF16),
        "sink_a": jnp.broadcast_to(jnp.repeat(sink_a.astype(F32) * LOG2E, BLOCK, axis=1)[:, :, None],
                                   (DEPTH, H_A * BLOCK, LANES)).reshape(DEPTH, KV_A, GQA * BLOCK, LANES),
        "bias_a": _band_bias(rel_bias, list(range(H_A)), 1, WIN_A, WIN_A),
        "bias_c": [_band_bias(rel_bias, heads_c[g], d, C_RADIUS, C_RADIUS) for g, (w, d) in enumerate(C_PATTERNS)],
    }


def kernel(x_prompt, x_sample, ln_g, w_in, q_norm_a, k_norm_a, sink_a, q_norm_b, k_norm_b,
           q_norm_c, k_norm_c, rel_bias, w_out):
    p = _prepare_params(ln_g, w_in, q_norm_a, k_norm_a, sink_a, q_norm_b, k_norm_b, q_norm_c, k_norm_c,
                        rel_bias, w_out)
    return _trunk(x_prompt, p), _trunk(x_sample, p)
```

```python
import functools
import math

import numpy as np
import jax
import jax.numpy as jnp
from jax import lax
from jax.experimental import pallas as pl
from jax.experimental.pallas import tpu as pltpu

F32 = jnp.float32
BF16 = jnp.bfloat16

D_MODEL = 2048
DEPTH = 2
HEAD_DIM = 64
N_HEADS_TOTAL = D_MODEL // HEAD_DIM
H_C = N_HEADS_TOTAL // 4
H_A = (N_HEADS_TOTAL - H_C) // 2
H_B = N_HEADS_TOTAL - H_C - H_A
KV_A = H_A // 4
KV_B = H_B // 4
KV_C = H_C // 4
GQA = 4
C_PATTERNS = ((128, 1), (512, 4), (2048, 16))
N_C_GROUPS = len(C_PATTERNS)
C_RADIUS = 64
assert all(w // (2 * d) == C_RADIUS for w, d in C_PATTERNS)
BLOCK = 128
LANES = 128
WIN_A = 128
GRID_W = 64
ROPE_THETA = 10000.0
T5_BUCKETS = 32
T5_MAX_DISTANCE = 1024
RMS_EPS = 1e-6
NEG_INF = -1e30
LOG2E = math.log2(math.e)
Q_SCALE = HEAD_DIM ** -0.5 * LOG2E

W_QG = GQA * HEAD_DIM
W_KV = 2 * HEAD_DIM
W_C = H_C * HEAD_DIM

OFF_QA = 0
OFF_QB = OFF_QA + H_A * HEAD_DIM
OFF_QC = OFF_QB + H_B * HEAD_DIM
OFF_KVA = OFF_QC + N_C_GROUPS * W_C
OFF_KVB = OFF_KVA + KV_A * W_KV
OFF_KVC = OFF_KVB + KV_B * W_KV
W_QKV = OFF_KVC + N_C_GROUPS * KV_C * W_KV
OFF_GA = W_QKV
OFF_GB = OFF_GA + H_A * HEAD_DIM
OFF_GC = OFF_GB + H_B * HEAD_DIM
D_IN_PROJ = OFF_GC + W_C

M_QA = 0
M_QB = M_QA + H_A * HEAD_DIM
M_QC0 = M_QB + H_B * HEAD_DIM
M_KVA = M_QC0 + W_C
M_KVB = M_KVA + KV_A * W_KV
M_KVC0 = M_KVB + KV_B * W_KV
W_MAIN = M_KVC0 + KV_C * W_KV
W_XC = W_C + KV_C * W_KV

VMEM_LIMIT = 48 * 1024 * 1024
VMEM_LIMIT_OUTPROJ = 54 * 1024 * 1024
OUTPROJ_PIECES = 2
BAND_ROWS = 1024


def _proj_column_order():
    sizes = (H_A * 64, KV_A * 64, KV_A * 64, H_A * 64, H_B * 64, KV_B * 64, KV_B * 64, H_B * 64,
             N_C_GROUPS * H_C * 64, N_C_GROUPS * KV_C * 64, N_C_GROUPS * KV_C * 64, H_C * 64)
    o = np.concatenate([[0], np.cumsum(sizes)])
    qa, ka, va, ga, qb, kb, vb, gb, qc, kc, vc, gc = (np.arange(o[i], o[i + 1]) for i in range(12))

    def kv_pairs(k, v, n):
        return np.concatenate([np.concatenate([k[h * 64:(h + 1) * 64], v[h * 64:(h + 1) * 64]]) for h in range(n)])

    order = np.concatenate([qa, qb, qc, kv_pairs(ka, va, KV_A), kv_pairs(kb, vb, KV_B),
                            kv_pairs(kc, vc, N_C_GROUPS * KV_C), ga, gb, gc])
    assert order.shape == (D_IN_PROJ,) and np.array_equal(np.sort(order), np.arange(D_IN_PROJ))
    return order


def _permute_columns(w, order):
    cuts = np.flatnonzero(np.diff(order) != 1) + 1
    starts = np.concatenate([[0], cuts])
    stops = np.concatenate([cuts, [order.size]])
    return jnp.concatenate([w[..., int(order[a]):int(order[b - 1]) + 1] for a, b in zip(starts, stops)], axis=-1)


def _t5_bucket(rel):
    half = T5_BUCKETS // 2
    max_exact = half // 2
    n = np.abs(rel)
    log_ratio = np.log(np.maximum(n, 1).astype(np.float64) / max_exact) / math.log(T5_MAX_DISTANCE / max_exact)
    large = np.minimum(max_exact + (log_ratio * (half - max_exact)).astype(np.int32), half - 1)
    return np.where(rel > 0, half, 0) + np.where(n < max_exact, n, large)


def _band_bias(rel_bias, heads, dilation, radius, halo):
    width = BLOCK + 2 * halo
    n = width + BLOCK - 1
    rel = np.arange(n) - (halo + BLOCK - 1)
    tab = rel_bias[_t5_bucket(rel * dilation)][:, heads[0]:heads[0] + len(heads)].astype(F32) * LOG2E
    tab = jnp.where((np.abs(rel) <= radius)[:, None], tab, NEG_INF)
    stream = jnp.tile(jnp.pad(tab, ((0, 1), (0, 0))), (BLOCK, 1))[:BLOCK * n]
    b = stream.reshape(BLOCK, n, len(heads))[:, BLOCK - 1:, :]
    return b.transpose(2, 0, 1).reshape(len(heads) // GQA, GQA * BLOCK, width)


def _rms(x, w):
    return x * lax.rsqrt(jnp.mean(x * x, axis=-1, keepdims=True) + RMS_EPS) * w


def _stack_heads(x):
    return jnp.concatenate([x[:, HEAD_DIM * r:HEAD_DIM * (r + 1)] for r in range(GQA)], axis=0)


def _unstack_heads(x):
    rows = x.shape[0] // GQA
    return jnp.concatenate([x[rows * r:rows * (r + 1)] for r in range(GQA)], axis=1)


def _silu(g):
    return g * jax.nn.sigmoid(g)


def _nt_dot(a, b):
    return lax.dot_general(a, b, (((1,), (1,)), ((), ())), preferred_element_type=F32)


def _inproj_kernel(x_ref, g_ref, w_ref, o_ref, h_ref):
    @pl.when(pl.program_id(1) == 0)
    def _():
        x = x_ref[...]
        h_ref[...] = _rms(x, g_ref[...]).astype(BF16)

    o_ref[...] = jnp.dot(h_ref[...], w_ref[...], preferred_element_type=F32).astype(o_ref.dtype)


def _inproj(x2, g, w, *, tm=1024, tn=D_IN_PROJ // 4):
    T = x2.shape[0]
    return pl.pallas_call(
        _inproj_kernel,
        grid=(T // tm, D_IN_PROJ // tn),
        in_specs=[pl.BlockSpec((tm, D_MODEL), lambda i, j: (i, 0)),
                  pl.BlockSpec((1, D_MODEL), lambda i, j: (0, 0)),
                  pl.BlockSpec((D_MODEL, tn), lambda i, j: (0, j))],
        out_specs=pl.BlockSpec((tm, tn), lambda i, j: (i, j)),
        out_shape=jax.ShapeDtypeStruct((T, D_IN_PROJ), BF16),
        scratch_shapes=[pltpu.VMEM((tm, D_MODEL), BF16)],
        compiler_params=pltpu.CompilerParams(dimension_semantics=("parallel", "arbitrary"),
                                             vmem_limit_bytes=VMEM_LIMIT),
        name="inproj",
    )(x2, g, w)


def _prep_plan():
    plan = []
    for cb in range(W_QKV // LANES):
        col = cb * LANES
        if col < OFF_QB:
            plan.append((False, False, "main", M_QA + col - OFF_QA))
        elif col < OFF_QC:
            plan.append((False, True, "main", M_QB + col - OFF_QB))
        elif col < OFF_KVA:
            g, off = divmod(col - OFF_QC, W_C)
            plan.append((False, False, "main", M_QC0 + off) if g == 0 else (False, False, f"xc{g}", off))
        elif col < OFF_KVB:
            plan.append((True, False, "main", M_KVA + col - OFF_KVA))
        elif col < OFF_KVC:
            plan.append((True, True, "main", M_KVB + col - OFF_KVB))
        else:
            g, off = divmod(col - OFF_KVC, KV_C * W_KV)
            plan.append((True, False, "main", M_KVC0 + off) if g == 0 else (True, False, f"xc{g}", W_C + off))
    return plan


def _prep_kernel(x_ref, w_ref, rope_ref, ones_ref, main_ref, xc1_ref, xc2_ref, stage_ref, *, tp, plan):
    lane = lax.broadcasted_iota(jnp.int32, (tp, LANES), 1)
    is_k = lane < HEAD_DIM
    swap_hi = (lane & 16) != 0
    seg_mean = ones_ref[...]
    n_stage = W_XC // LANES
    for cb, (is_kv, rotary, dest, col) in enumerate(plan):
        cols = slice(cb * LANES, (cb + 1) * LANES)
        x = x_ref[0, :, cols].astype(F32)
        inv = lax.rsqrt(jnp.dot((x * x).astype(BF16), seg_mean, preferred_element_type=F32) + RMS_EPS)
        if is_kv:
            inv = jnp.where(is_k, inv, 1.0)
        y = x * inv * w_ref[:, cols]
        if rotary:
            base = 2 * LANES if is_kv else 0
            cos = rope_ref[:, base:base + LANES]
            sin_signed = rope_ref[:, base + LANES:base + 2 * LANES]
            swapped = jnp.where(swap_hi, pltpu.roll(y, 16, 1), pltpu.roll(y, LANES - 16, 1))
            y = y * cos + swapped * sin_signed
        if dest == "main":
            main_ref[0, :, col:col + LANES] = y.astype(BF16)
        else:
            stage_ref[(int(dest[2]) - 1) * n_stage + col // LANES] = y
    for g, xc_ref in ((1, xc1_ref), (2, xc2_ref)):
        d = C_PATTERNS[g][1]
        for r in range(d):
            for k in range(n_stage):
                piece = stage_ref[(g - 1) * n_stage + k, pl.ds(r, tp // d, stride=d), :]
                xc_ref[0, r, :, k * LANES:(k + 1) * LANES] = piece.astype(BF16)


def _prepare(proj, wvec, rope, seg_ones, *, tp=512):
    B, S, _ = proj.shape
    d1, d2 = C_PATTERNS[1][1], C_PATTERNS[2][1]
    plan = _prep_plan()
    return pl.pallas_call(
        functools.partial(_prep_kernel, tp=tp, plan=plan),
        grid=(B, S // tp),
        in_specs=[pl.BlockSpec((1, tp, W_QKV), lambda b, i: (b, i, 0)),
                  pl.BlockSpec((1, W_QKV), lambda b, i: (0, 0)),
                  pl.BlockSpec((tp, 4 * LANES), lambda b, i: (i, 0)),
                  pl.BlockSpec((LANES, LANES), lambda b, i: (0, 0))],
        out_specs=[pl.BlockSpec((1, tp, W_MAIN), lambda b, i: (b, i, 0)),
                   pl.BlockSpec((1, d1, tp // d1, W_XC), lambda b, i: (b, 0, i, 0)),
                   pl.BlockSpec((1, d2, tp // d2, W_XC), lambda b, i: (b, 0, i, 0))],
        out_shape=[jax.ShapeDtypeStruct((B, S, W_MAIN), BF16),
                   jax.ShapeDtypeStruct((B, d1, S // d1, W_XC), BF16),
                   jax.ShapeDtypeStruct((B, d2, S // d2, W_XC), BF16)],
        scratch_shapes=[pltpu.VMEM((2 * (W_XC // LANES), tp, LANES), F32)],
        compiler_params=pltpu.CompilerParams(dimension_semantics=("parallel", "parallel"),
                                             vmem_limit_bytes=VMEM_LIMIT),
        name="prepare",
    )(proj, wvec, rope, seg_ones)


def _band_kernel(*refs, nsub, blk_axis, with_sink, halo, classes):
    if classes is None:
        _band_rows(*refs, nsub=nsub, blk_axis=blk_axis, with_sink=with_sink, halo=halo)
        return
    q_ref, kp_ref, kc_ref, kn_ref, bias_ref, o_ref, l_ref = refs
    for c in range(classes):
        _band_rows(q_ref.at[c], kp_ref.at[c], kc_ref.at[c], kn_ref.at[c], bias_ref, o_ref.at[c], l_ref.at[c],
                   nsub=nsub, blk_axis=blk_axis, with_sink=with_sink, halo=halo)


def _band_rows(*refs, nsub, blk_axis, with_sink, halo):
    if with_sink:
        q_ref, kp_ref, kc_ref, kn_ref, bias_ref, sink_ref, gate_ref, o_ref = refs
    else:
        q_ref, kp_ref, kc_ref, kn_ref, bias_ref, o_ref, l_ref = refs
    i = pl.program_id(blk_axis)
    nb = pl.num_programs(blk_axis)
    kv = jnp.concatenate([kp_ref[...], kc_ref[...], kn_ref[...]], axis=0)
    bias = bias_ref[0]
    width = BLOCK + 2 * halo
    col = lax.broadcasted_iota(jnp.int32, bias.shape, 1)
    for t in range(nsub):
        rows = slice(t * BLOCK, (t + 1) * BLOCK)
        q4 = _stack_heads(q_ref[rows, :])
        start = (t + 1) * BLOCK - halo
        win = kv[start:start + width]
        s = _nt_dot(q4, win[:, :HEAD_DIM]) + bias
        if t == 0:
            s = jnp.where((col < halo) & (i == 0), NEG_INF, s)
        if t == nsub - 1:
            s = jnp.where((col >= BLOCK + halo) & (i == nb - 1), NEG_INF, s)
        m = jnp.broadcast_to(jnp.max(s, axis=-1, keepdims=True), (GQA * BLOCK, LANES))
        if with_sink:
            sink = sink_ref[0]
            m = jnp.maximum(m, sink)
        e = jnp.exp2(s - jnp.concatenate([m] * (width // LANES), axis=1)).astype(BF16)
        pv = jnp.dot(e, jnp.concatenate([win, jnp.ones((width, LANES), BF16)], axis=1), preferred_element_type=F32)
        denom = pv[:, LANES:]
        if with_sink:
            denom = denom + jnp.exp2(sink - m)
        o = pv[:, :LANES] / denom
        low = lax.broadcasted_iota(jnp.int32, (BLOCK, LANES), 1) < HEAD_DIM

        def pair_heads(x, p):
            even, odd = x[2 * p * BLOCK:(2 * p + 1) * BLOCK], x[(2 * p + 1) * BLOCK:(2 * p + 2) * BLOCK]
            return even, odd

        pairs = []
        for p in range(GQA // 2):
            even, odd = pair_heads(o, p)
            pairs.append(jnp.where(low, pltpu.roll(even, HEAD_DIM, 1), odd))
        o2 = jnp.concatenate(pairs, axis=1)
        if with_sink:
            g = gate_ref[rows, :].astype(F32)
            o_ref[rows, :] = (o2 * _silu(g)).astype(o_ref.dtype)
        else:
            o_ref[rows, :] = o2
            lse = m + jnp.log2(denom)
            l_ref[rows, :] = jnp.concatenate([jnp.where(low, *pair_heads(lse, p)) for p in range(GQA // 2)], axis=1)


def _band_specs(lead_block, lead_of, tq, nblk, q_blk, kv_blk):
    ratio = tq // BLOCK

    def spec(rows, width, row_fn, col_fn):
        return pl.BlockSpec(lead_block + (rows, width), lambda *g: lead_of(*g) + (row_fn(g[-1]), col_fn(*g)))

    return [spec(tq, W_QG, lambda u: u, q_blk),
            spec(BLOCK, W_KV, lambda u: jnp.maximum(u * ratio - 1, 0), kv_blk),
            spec(tq, W_KV, lambda u: u, kv_blk),
            spec(BLOCK, W_KV, lambda u: jnp.minimum((u + 1) * ratio, nblk * ratio - 1), kv_blk)]


def _window_sink_attention(main, proj, bias, sink):
    B, S, _ = main.shape
    tq = min(BAND_ROWS, S)
    nblk = S // tq
    lead_of = lambda b, h, u: (b,)
    in_specs = _band_specs((None,), lead_of, tq, nblk,
                           lambda b, h, u: M_QA // W_QG + h, lambda b, h, u: M_KVA // W_KV + h)
    in_specs += [pl.BlockSpec((1, GQA * BLOCK, BLOCK + 2 * WIN_A), lambda b, h, u: (h, 0, 0)),
                 pl.BlockSpec((1, GQA * BLOCK, LANES), lambda b, h, u: (h, 0, 0)),
                 pl.BlockSpec((None, tq, W_QG), lambda b, h, u: (b, u, OFF_GA // W_QG + h))]
    return pl.pallas_call(
        functools.partial(_band_kernel, nsub=tq // BLOCK, blk_axis=2, with_sink=True, halo=WIN_A, classes=None),
        grid=(B, KV_A, nblk),
        in_specs=in_specs,
        out_specs=pl.BlockSpec((None, tq, W_QG), lambda b, h, u: (b, u, h)),
        out_shape=jax.ShapeDtypeStruct((B, S, H_A * HEAD_DIM), BF16),
        compiler_params=pltpu.CompilerParams(dimension_semantics=("parallel",) * 3, vmem_limit_bytes=VMEM_LIMIT),
        name="mixer_a",
    )(main, main, main, main, bias, sink, proj)


def _dilated_group_attention(x, bias, group):
    if group == 0:
        B, sub, _ = x.shape
        tq = min(BAND_ROWS, sub)
        d, classes, lead_block = 1, None, (None,)
        lead_of = lambda b, r, j, u: (b,)
        q0, kv0 = M_QC0 // W_QG, M_KVC0 // W_KV
        out_shape = jax.ShapeDtypeStruct((B, sub, W_C), F32)
    else:
        B, d, sub, _ = x.shape
        tq = min(BAND_ROWS, sub)
        classes = min(d, BAND_ROWS // tq)
        lead_block = (None, classes)
        lead_of = lambda b, r, j, u: (b, r)
        q0, kv0 = 0, W_C // W_KV
        out_shape = jax.ShapeDtypeStruct((B, d, sub, W_C), F32)
    nblk = sub // tq
    in_specs = _band_specs(lead_block, lead_of, tq, nblk, lambda b, r, j, u: q0 + j, lambda b, r, j, u: kv0 + j)
    in_specs += [pl.BlockSpec((1, GQA * BLOCK, BLOCK + 2 * C_RADIUS), lambda b, r, j, u: (j, 0, 0))]
    out_spec = pl.BlockSpec(lead_block + (tq, W_QG), lambda b, r, j, u: lead_of(b, r, j, u) + (u, j))
    return pl.pallas_call(
        functools.partial(_band_kernel, nsub=tq // BLOCK, blk_axis=3, with_sink=False, halo=C_RADIUS,
                          classes=classes),
        grid=(B, d // (classes or 1), KV_C, nblk),
        in_specs=in_specs,
        out_specs=[out_spec, out_spec],
        out_shape=[out_shape, out_shape],
        compiler_params=pltpu.CompilerParams(dimension_semantics=("parallel",) * 4, vmem_limit_bytes=VMEM_LIMIT),
        name=f"mixer_c{group}",
    )(x, x, x, x, bias)


def _stage_residue_classes(o1_ref, l1_ref, o2_ref, l2_ref, stage_ref, tm):
    nk = W_C // LANES
    for a, (ref, g) in enumerate(((o1_ref, 1), (l1_ref, 1), (o2_ref, 2), (l2_ref, 2))):
        d = C_PATTERNS[g][1]
        for r in range(d):
            for k in range(nk):
                stage_ref[a * nk + k, pl.ds(r, tm // d, stride=d), :] = ref[0, r, :, k * LANES:(k + 1) * LANES]


def _merge_groups_tile(o0_ref, l0_ref, gate_ref, stage_ref, rows):
    nk = W_C // LANES
    tiles = []
    for k in range(nk):
        cols = slice(k * LANES, (k + 1) * LANES)
        o0, l0 = o0_ref[0, rows, cols], l0_ref[0, rows, cols]
        o1, l1 = stage_ref[k, rows, :], stage_ref[nk + k, rows, :]
        o2, l2 = stage_ref[2 * nk + k, rows, :], stage_ref[3 * nk + k, rows, :]
        m = jnp.maximum(jnp.maximum(l0, l1), l2)
        w0, w1, w2 = jnp.exp2(l0 - m), jnp.exp2(l1 - m), jnp.exp2(l2 - m)
        y = (w0 * o0 + w1 * o1 + w2 * o2) / (w0 + w1 + w2)
        tiles.append((y * _silu(gate_ref[0, rows, cols].astype(F32))).astype(BF16))
    return jnp.concatenate(tiles, axis=1)


def _global_kernel(q_ref, kv_ref, gate_ref, o_ref, v_s, *, seq, tk, prep_rows):
    @pl.when(pl.program_id(2) == 0)
    def _stage_values():
        def body(c, carry):
            rows = pl.ds(pl.multiple_of(c * prep_rows, prep_rows), prep_rows)
            v_s[rows, :] = jnp.concatenate([kv_ref[rows, HEAD_DIM:], jnp.ones((prep_rows, HEAD_DIM), BF16)], axis=1)
            return carry

        lax.fori_loop(0, seq // prep_rows, body, 0)

    q4 = _stack_heads(q_ref[...])
    rows4 = q4.shape[0]
    m = jnp.full((rows4, 1), NEG_INF, F32)
    acc = jnp.zeros((rows4, 2 * HEAD_DIM), F32)
    for c in range(seq // tk):
        rows = slice(c * tk, (c + 1) * tk)
        s = _nt_dot(q4, kv_ref[rows, :HEAD_DIM])
        m_new = jnp.maximum(m, jnp.max(s, axis=-1, keepdims=True))
        p = jnp.exp2(s - m_new).astype(BF16)
        acc = jnp.exp2(m - m_new) * acc + jnp.dot(p, v_s[rows, :], preferred_element_type=F32)
        m = m_new
    o = acc[:, :HEAD_DIM] / acc[:, HEAD_DIM:HEAD_DIM + 1]
    g = gate_ref[...].astype(F32)
    o_ref[...] = (_unstack_heads(o) * _silu(g)).astype(o_ref.dtype)


def _global_axial_attention(main, proj, *, tq=512, tk=512):
    B, S, _ = main.shape
    return pl.pallas_call(
        functools.partial(_global_kernel, seq=S, tk=tk, prep_rows=512),
        grid=(B, KV_B, S // tq),
        in_specs=[pl.BlockSpec((None, tq, W_QG), lambda b, h, i: (b, i, M_QB // W_QG + h)),
                  pl.BlockSpec((None, S, W_KV), lambda b, h, i: (b, 0, M_KVB // W_KV + h)),
                  pl.BlockSpec((None, tq, W_QG), lambda b, h, i: (b, i, OFF_GB // W_QG + h))],
        out_specs=pl.BlockSpec((None, tq, W_QG), lambda b, h, i: (b, i, h)),
        out_shape=jax.ShapeDtypeStruct((B, S, H_B * HEAD_DIM), BF16),
        scratch_shapes=[pltpu.VMEM((S, 2 * HEAD_DIM), BF16)],
        compiler_params=pltpu.CompilerParams(dimension_semantics=("parallel", "parallel", "arbitrary"),
                                             vmem_limit_bytes=VMEM_LIMIT),
        name="mixer_b",
    )(main, main, proj)


def _outproj_kernel(x_ref, a_ref, b_ref, o0_ref, l0_ref, o1_ref, l1_ref, o2_ref, l2_ref, gate_ref,
                    wa_ref, wb_ref, wc_ref, o_ref, stage_ref, *, tm):
    _stage_residue_classes(o1_ref, l1_ref, o2_ref, l2_ref, stage_ref, tm)
    piece = tm // OUTPROJ_PIECES
    for h in range(OUTPROJ_PIECES):
        rows = slice(h * piece, (h + 1) * piece)
        y = jnp.dot(a_ref[0, rows, :], wa_ref[...], preferred_element_type=F32)
        y += jnp.dot(b_ref[0, rows, :], wb_ref[...], preferred_element_type=F32)
        mix_c = _merge_groups_tile(o0_ref, l0_ref, gate_ref, stage_ref, rows)
        y += jnp.dot(mix_c, wc_ref[...], preferred_element_type=F32)
        o_ref[0, rows, :] = x_ref[0, rows, :] + y


def _outproj(x, mix_a, mix_b, outs, lses, proj, w, *, tm=512):
    B, S, _ = x.shape
    ka, kb, kc = H_A * HEAD_DIM, H_B * HEAD_DIM, W_C
    d1, d2 = C_PATTERNS[1][1], C_PATTERNS[2][1]

    def tok(width, col=0):
        return pl.BlockSpec((1, tm, width), lambda b, i: (b, i, col))

    def weight(rows, blk):
        return pl.BlockSpec((rows, D_MODEL), lambda b, i: (blk, 0), pipeline_mode=pl.Buffered(1))

    res1 = pl.BlockSpec((1, d1, tm // d1, kc), lambda b, i: (b, 0, i, 0))
    res2 = pl.BlockSpec((1, d2, tm // d2, kc), lambda b, i: (b, 0, i, 0))
    return pl.pallas_call(
        functools.partial(_outproj_kernel, tm=tm),
        grid=(B, S // tm),
        in_specs=[tok(D_MODEL), tok(ka), tok(kb), tok(kc), tok(kc), res1, res1, res2, res2, tok(kc, OFF_GC // kc),
                  weight(ka, 0), weight(kb, 1), weight(kc, (ka + kb) // kc)],
        out_specs=tok(D_MODEL),
        out_shape=jax.ShapeDtypeStruct((B, S, D_MODEL), F32),
        scratch_shapes=[pltpu.VMEM((4 * (kc // LANES), tm, LANES), F32)],
        compiler_params=pltpu.CompilerParams(dimension_semantics=("parallel", "parallel"),
                                             vmem_limit_bytes=VMEM_LIMIT_OUTPROJ),
        name="outproj",
    )(x, mix_a, mix_b, outs[0], lses[0], outs[1], lses[1], outs[2], lses[2], proj, w, w, w)


def _rope_table(S):
    rows = S // GRID_W
    row = jnp.repeat(jnp.arange(rows, dtype=jnp.int32), GRID_W)
    col = jnp.arange(S, dtype=jnp.int32) % GRID_W
    n_freq = HEAD_DIM // 4
    inv_freq = ROPE_THETA ** (-jnp.arange(n_freq, dtype=F32) / n_freq)
    ang_row = row.astype(F32)[:, None] * inv_freq[None, :]
    ang_col = col.astype(F32)[:, None] * inv_freq[None, :]
    cr, sr, cc, sc = jnp.cos(ang_row), jnp.sin(ang_row), jnp.cos(ang_col), jnp.sin(ang_col)
    cos = jnp.concatenate([cr, cr, cc, cc], axis=1)
    sin_signed = jnp.concatenate([-sr, sr, -sc, sc], axis=1)
    one, zero = jnp.ones_like(cos), jnp.zeros_like(cos)
    return jnp.concatenate([cos, cos, sin_signed, sin_signed, cos, one, sin_signed, zero], axis=1)


def _trunk(x, p):
    B, S, _ = x.shape
    rope = _rope_table(S)
    for l in range(DEPTH):
        proj = _inproj(x.reshape(B * S, D_MODEL), p["ln_g"][l], p["w_in"][l]).reshape(B, S, D_IN_PROJ)
        main, xc1, xc2 = _prepare(proj, p["wvec"][l], rope, p["seg_ones"])
        mix_a = _window_sink_attention(main, proj, p["bias_a"], p["sink_a"][l])
        mix_b = _global_axial_attention(main, proj)
        outs, lses = zip(*[_dilated_group_attention(xg, p["bias_c"][g], g) for g, xg in enumerate((main, xc1, xc2))])
        x = _outproj(x, mix_a, mix_b, outs, lses, proj, p["w_out"][l])
    return x


def _prepare_params(ln_g, w_in, q_norm_a, k_norm_a, sink_a, q_norm_b, k_norm_b, q_norm_c, k_norm_c, rel_bias, w_out):
    heads_c = [list(range(H_A + g * H_C, H_A + (g + 1) * H_C)) for g in range(N_C_GROUPS)]
    one = jnp.ones((DEPTH, HEAD_DIM), F32)

    def q_cols(w, n):
        return jnp.tile(w.astype(F32) * Q_SCALE, (1, n))

    def kv_cols(w, n):
        return jnp.tile(jnp.concatenate([w.astype(F32), one], axis=1), (1, n))

    wvec = jnp.concatenate([q_cols(q_norm_a, H_A), q_cols(q_norm_b, H_B), q_cols(q_norm_c, N_C_GROUPS * H_C),
                            kv_cols(k_norm_a, KV_A), kv_cols(k_norm_b, KV_B), kv_cols(k_norm_c, N_C_GROUPS * KV_C)],
                           axis=1)
    seg = np.arange(LANES) // HEAD_DIM
    return {
        "ln_g": ln_g.reshape(DEPTH, 1, D_MODEL),
        "w_in": _permute_columns(w_in, _proj_column_order()).astype(BF16),
        "w_out": w_out.astype(BF16),
        "wvec": wvec.reshape(DEPTH, 1, W_QKV),
        "seg_ones": jnp.asarray((seg[:, None] == seg[None, :]) / HEAD_DIM, BF16),
        "sink_a": jnp.broadcast_to(jnp.repeat(sink_a.astype(F32) * LOG2E, BLOCK, axis=1)[:, :, None],
                                   (DEPTH, H_A * BLOCK, LANES)).reshape(DEPTH, KV_A, GQA * BLOCK, LANES),
        "bias_a": _band_bias(rel_bias, list(range(H_A)), 1, WIN_A, WIN_A),
        "bias_c": [_band_bias(rel_bias, heads_c[g], d, C_RADIUS, C_RADIUS) for g, (w, d) in enumerate(C_PATTERNS)],
    }


def kernel(x_prompt, x_sample, ln_g, w_in, q_norm_a, k_norm_a, sink_a, q_norm_b, k_norm_b,
           q_norm_c, k_norm_c, rel_bias, w_out):
    p = _prepare_params(ln_g, w_in, q_norm_a, k_norm_a, sink_a, q_norm_b, k_norm_b, q_norm_c, k_norm_c,
                        rel_bias, w_out)
    return _trunk(x_prompt, p), _trunk(x_sample, p)
```

```python
import functools
import math

import numpy as np
import jax
import jax.numpy as jnp
from jax import lax
from jax.experimental import pallas as pl
from jax.experimental.pallas import tpu as pltpu

F32 = jnp.float32
BF16 = jnp.bfloat16

D_MODEL = 2048
DEPTH = 2
HEAD_DIM = 64
N_HEADS_TOTAL = D_MODEL // HEAD_DIM
H_C = N_HEADS_TOTAL // 4
H_A = (N_HEADS_TOTAL - H_C) // 2
H_B = N_HEADS_TOTAL - H_C - H_A
KV_A = H_A // 4
KV_B = H_B // 4
KV_C = H_C // 4
GQA = 4
C_PATTERNS = ((128, 1), (512, 4), (2048, 16))
N_C_GROUPS = len(C_PATTERNS)
C_RADIUS = 64
assert all(w // (2 * d) == C_RADIUS for w, d in C_PATTERNS)
BLOCK = 128
LANES = 128
WIN_A = 128
GRID_W = 64
ROPE_THETA = 10000.0
T5_BUCKETS = 32
T5_MAX_DISTANCE = 1024
RMS_EPS = 1e-6
NEG_INF = -1e30
LOG2E = math.log2(math.e)
Q_SCALE = HEAD_DIM ** -0.5 * LOG2E

W_QG = GQA * HEAD_DIM
W_KV = 2 * HEAD_DIM
W_C = H_C * HEAD_DIM

OFF_QA = 0
OFF_QB = OFF_QA + H_A * HEAD_DIM
OFF_QC = OFF_QB + H_B * HEAD_DIM
OFF_KVA = OFF_QC + N_C_GROUPS * W_C
OFF_KVB = OFF_KVA + KV_A * W_KV
OFF_KVC = OFF_KVB + KV_B * W_KV
W_QKV = OFF_KVC + N_C_GROUPS * KV_C * W_KV
OFF_GA = W_QKV
OFF_GB = OFF_GA + H_A * HEAD_DIM
OFF_GC = OFF_GB + H_B * HEAD_DIM
D_IN_PROJ = OFF_GC + W_C

M_QA = 0
M_QB = M_QA + H_A * HEAD_DIM
M_QC0 = M_QB + H_B * HEAD_DIM
M_KVA = M_QC0 + W_C
M_KVB = M_KVA + KV_A * W_KV
M_KVC0 = M_KVB + KV_B * W_KV
W_MAIN = M_KVC0 + KV_C * W_KV
W_XC = W_C + KV_C * W_KV

MIB = 1024 * 1024
VMEM_LIMIT = 48 * MIB
VMEM_LIMIT_OUTPROJ = 54 * MIB
VMEM_LIMIT_INPROJ = 56 * MIB
OUTPROJ_PIECES = 2
BAND_ROWS = 1024


def _proj_column_order():
    heads = (H_A, KV_A, KV_A, H_A, H_B, KV_B, KV_B, H_B,
             N_C_GROUPS * H_C, N_C_GROUPS * KV_C, N_C_GROUPS * KV_C, H_C)
    o = np.concatenate([[0], np.cumsum(heads)]) * HEAD_DIM
    qa, ka, va, ga, qb, kb, vb, gb, qc, kc, vc, gc = (np.arange(o[i], o[i + 1]) for i in range(12))

    def kv_pairs(k, v, n):
        head = lambda a, h: a[h * HEAD_DIM:(h + 1) * HEAD_DIM]
        return np.concatenate([np.concatenate([head(k, h), head(v, h)]) for h in range(n)])

    order = np.concatenate([qa, qb, qc, kv_pairs(ka, va, KV_A), kv_pairs(kb, vb, KV_B),
                            kv_pairs(kc, vc, N_C_GROUPS * KV_C), ga, gb, gc])
    assert order.shape == (D_IN_PROJ,) and np.array_equal(np.sort(order), np.arange(D_IN_PROJ))
    return order


def _permute_columns(w, order):
    cuts = np.flatnonzero(np.diff(order) != 1) + 1
    starts = np.concatenate([[0], cuts])
    stops = np.concatenate([cuts, [order.size]])
    return jnp.concatenate([w[..., int(order[a]):int(order[b - 1]) + 1] for a, b in zip(starts, stops)], axis=-1)


def _t5_bucket(rel):
    half = T5_BUCKETS // 2
    max_exact = half // 2
    n = np.abs(rel)
    log_ratio = np.log(np.maximum(n, 1).astype(np.float64) / max_exact) / math.log(T5_MAX_DISTANCE / max_exact)
    large = np.minimum(max_exact + (log_ratio * (half - max_exact)).astype(np.int32), half - 1)
    return np.where(rel > 0, half, 0) + np.where(n < max_exact, n, large)


def _band_bias(rel_bias, heads, dilation, radius, halo):
    width = BLOCK + 2 * halo
    n = width + BLOCK - 1
    rel = np.arange(n) - (halo + BLOCK - 1)
    tab = rel_bias[_t5_bucket(rel * dilation)][:, heads[0]:heads[0] + len(heads)].astype(F32) * LOG2E
    tab = jnp.where((np.abs(rel) <= radius)[:, None], tab, NEG_INF)
    stream = jnp.tile(jnp.pad(tab, ((0, 1), (0, 0))), (BLOCK, 1))[:BLOCK * n]
    b = stream.reshape(BLOCK, n, len(heads))[:, BLOCK - 1:, :]
    return b.transpose(2, 0, 1).reshape(len(heads) // GQA, GQA * BLOCK, width)


def _rms(x, w):
    return x * lax.rsqrt(jnp.mean(x * x, axis=-1, keepdims=True) + RMS_EPS) * w


def _stack_heads(x):
    return jnp.concatenate([x[:, HEAD_DIM * r:HEAD_DIM * (r + 1)] for r in range(GQA)], axis=0)


def _unstack_heads(x):
    rows = x.shape[0] // GQA
    return jnp.concatenate([x[rows * r:rows * (r + 1)] for r in range(GQA)], axis=1)


def _head_rows(x, r):
    return x[r * BLOCK:(r + 1) * BLOCK]


def _silu(g):
    return g * jax.nn.sigmoid(g)


def _nt_dot(a, b):
    return lax.dot_general(a, b, (((1,), (1,)), ((), ())), preferred_element_type=F32)


def _inproj_kernel(x_ref, g_ref, w_ref, o_ref):
    h = _rms(x_ref[...], g_ref[...]).astype(BF16)
    o_ref[...] = jnp.dot(h, w_ref[...], preferred_element_type=F32).astype(o_ref.dtype)


def _inproj(x2, g, w, *, tm=512):
    T = x2.shape[0]
    return pl.pallas_call(
        _inproj_kernel,
        grid=(T // tm,),
        in_specs=[pl.BlockSpec((tm, D_MODEL), lambda i: (i, 0)),
                  pl.BlockSpec((1, D_MODEL), lambda i: (0, 0)),
                  pl.BlockSpec((D_MODEL, D_IN_PROJ), lambda i: (0, 0), pipeline_mode=pl.Buffered(1))],
        out_specs=pl.BlockSpec((tm, D_IN_PROJ), lambda i: (i, 0)),
        out_shape=jax.ShapeDtypeStruct((T, D_IN_PROJ), BF16),
        compiler_params=pltpu.CompilerParams(dimension_semantics=("parallel",),
                                             vmem_limit_bytes=VMEM_LIMIT_INPROJ),
        name="inproj",
    )(x2, g, w)


def _prep_plan():
    plan = []
    for cb in range(W_QKV // LANES):
        col = cb * LANES
        if col < OFF_QB:
            plan.append((False, False, "main", M_QA + col - OFF_QA))
        elif col < OFF_QC:
            plan.append((False, True, "main", M_QB + col - OFF_QB))
        elif col < OFF_KVA:
            g, off = divmod(col - OFF_QC, W_C)
            plan.append((False, False, "main", M_QC0 + off) if g == 0 else (False, False, f"xc{g}", off))
        elif col < OFF_KVB:
            plan.append((True, False, "main", M_KVA + col - OFF_KVA))
        elif col < OFF_KVC:
            plan.append((True, True, "main", M_KVB + col - OFF_KVB))
        else:
            g, off = divmod(col - OFF_KVC, KV_C * W_KV)
            plan.append((True, False, "main", M_KVC0 + off) if g == 0 else (True, False, f"xc{g}", W_C + off))
    return plan


def _prep_kernel(x_ref, w_ref, rope_ref, ones_ref, main_ref, xc1_ref, xc2_ref, stage_ref, *, tp, plan):
    lane = lax.broadcasted_iota(jnp.int32, (tp, LANES), 1)
    is_k = lane < HEAD_DIM
    swap_hi = (lane & 16) != 0
    seg_mean = ones_ref[...]
    n_stage = W_XC // LANES
    for cb, (is_kv, rotary, dest, col) in enumerate(plan):
        cols = slice(cb * LANES, (cb + 1) * LANES)
        x = x_ref[0, :, cols].astype(F32)
        inv = lax.rsqrt(jnp.dot((x * x).astype(BF16), seg_mean, preferred_element_type=F32) + RMS_EPS)
        if is_kv:
            inv = jnp.where(is_k, inv, 1.0)
        y = x * inv * w_ref[:, cols]
        if rotary:
            base = 2 * LANES if is_kv else 0
            cos = rope_ref[:, base:base + LANES]
            sin_signed = rope_ref[:, base + LANES:base + 2 * LANES]
            swapped = jnp.where(swap_hi, pltpu.roll(y, 16, 1), pltpu.roll(y, LANES - 16, 1))
            y = y * cos + swapped * sin_signed
        if dest == "main":
            main_ref[0, :, col:col + LANES] = y.astype(BF16)
        else:
            stage_ref[(int(dest[2]) - 1) * n_stage + col // LANES] = y
    for g, xc_ref in ((1, xc1_ref), (2, xc2_ref)):
        d = C_PATTERNS[g][1]
        for r in range(d):
            for k in range(n_stage):
                piece = stage_ref[(g - 1) * n_stage + k, pl.ds(r, tp // d, stride=d), :]
                xc_ref[0, r, :, k * LANES:(k + 1) * LANES] = piece.astype(BF16)


def _prepare(proj, wvec, rope, seg_ones, *, tp=512):
    B, S, _ = proj.shape
    d1, d2 = C_PATTERNS[1][1], C_PATTERNS[2][1]
    plan = _prep_plan()
    return pl.pallas_call(
        functools.partial(_prep_kernel, tp=tp, plan=plan),
        grid=(B, S // tp),
        in_specs=[pl.BlockSpec((1, tp, W_QKV), lambda b, i: (b, i, 0)),
                  pl.BlockSpec((1, W_QKV), lambda b, i: (0, 0)),
                  pl.BlockSpec((tp, 4 * LANES), lambda b, i: (i, 0)),
                  pl.BlockSpec((LANES, LANES), lambda b, i: (0, 0))],
        out_specs=[pl.BlockSpec((1, tp, W_MAIN), lambda b, i: (b, i, 0)),
                   pl.BlockSpec((1, d1, tp // d1, W_XC), lambda b, i: (b, 0, i, 0)),
                   pl.BlockSpec((1, d2, tp // d2, W_XC), lambda b, i: (b, 0, i, 0))],
        out_shape=[jax.ShapeDtypeStruct((B, S, W_MAIN), BF16),
                   jax.ShapeDtypeStruct((B, d1, S // d1, W_XC), BF16),
                   jax.ShapeDtypeStruct((B, d2, S // d2, W_XC), BF16)],
        scratch_shapes=[pltpu.VMEM((2 * (W_XC // LANES), tp, LANES), F32)],
        compiler_params=pltpu.CompilerParams(dimension_semantics=("parallel", "parallel"),
                                             vmem_limit_bytes=VMEM_LIMIT),
        name="prepare",
    )(proj, wvec, rope, seg_ones)


def _band_kernel(*refs, nsub, blk_axis, with_sink, halo, classes):
    if classes is None:
        _band_rows(*refs, nsub=nsub, blk_axis=blk_axis, with_sink=with_sink, halo=halo)
        return
    q_ref, kp_ref, kc_ref, kn_ref, bias_ref, o_ref, l_ref = refs
    for c in range(classes):
        _band_rows(q_ref.at[c], kp_ref.at[c], kc_ref.at[c], kn_ref.at[c], bias_ref, o_ref.at[c], l_ref.at[c],
                   nsub=nsub, blk_axis=blk_axis, with_sink=with_sink, halo=halo)


def _band_rows(*refs, nsub, blk_axis, with_sink, halo):
    if with_sink:
        q_ref, kp_ref, kc_ref, kn_ref, bias_ref, sink_ref, gate_ref, o_ref = refs
    else:
        q_ref, kp_ref, kc_ref, kn_ref, bias_ref, o_ref, l_ref = refs
    i = pl.program_id(blk_axis)
    nb = pl.num_programs(blk_axis)
    kv = jnp.concatenate([kp_ref[...], kc_ref[...], kn_ref[...]], axis=0)
    bias = bias_ref[0]
    width = BLOCK + 2 * halo
    col = lax.broadcasted_iota(jnp.int32, bias.shape, 1)
    low = lax.broadcasted_iota(jnp.int32, (BLOCK, LANES), 1) < HEAD_DIM
    for t in range(nsub):
        rows = slice(t * BLOCK, (t + 1) * BLOCK)
        q4 = _stack_heads(q_ref[rows, :])
        start = (t + 1) * BLOCK - halo
        win = kv[start:start + width]
        s = _nt_dot(q4, win[:, :HEAD_DIM]) + bias
        if t == 0:
            s = jnp.where((col < halo) & (i == 0), NEG_INF, s)
        if t == nsub - 1:
            s = jnp.where((col >= BLOCK + halo) & (i == nb - 1), NEG_INF, s)
        m = jnp.broadcast_to(jnp.max(s, axis=-1, keepdims=True), (GQA * BLOCK, LANES))
        if with_sink:
            sink = sink_ref[0]
            m = jnp.maximum(m, sink)
        e = jnp.exp2(s - jnp.concatenate([m] * (width // LANES), axis=1)).astype(BF16)
        pv = jnp.dot(e, jnp.concatenate([win, jnp.ones((width, LANES), BF16)], axis=1), preferred_element_type=F32)
        denom = pv[:, LANES:]
        if with_sink:
            denom = denom + jnp.exp2(sink - m)
        o = pv[:, :LANES] / denom
        pairs = [jnp.where(low, pltpu.roll(_head_rows(o, 2 * p), HEAD_DIM, 1), _head_rows(o, 2 * p + 1))
                 for p in range(GQA // 2)]
        o2 = jnp.concatenate(pairs, axis=1)
        if with_sink:
            g = gate_ref[rows, :].astype(F32)
            o_ref[rows, :] = (o2 * _silu(g)).astype(o_ref.dtype)
        else:
            o_ref[rows, :] = o2
            lse = m + jnp.log2(denom)
            l_ref[rows, :] = jnp.concatenate(
                [jnp.where(low, _head_rows(lse, 2 * p), _head_rows(lse, 2 * p + 1)) for p in range(GQA // 2)], axis=1)


def _band_specs(lead_block, lead_of, tq, nblk, q_blk, kv_blk):
    ratio = tq // BLOCK

    def spec(rows, width, row_fn, col_fn):
        return pl.BlockSpec(lead_block + (rows, width), lambda *g: lead_of(*g) + (row_fn(g[-1]), col_fn(*g)))

    return [spec(tq, W_QG, lambda u: u, q_blk),
            spec(BLOCK, W_KV, lambda u: jnp.maximum(u * ratio - 1, 0), kv_blk),
            spec(tq, W_KV, lambda u: u, kv_blk),
            spec(BLOCK, W_KV, lambda u: jnp.minimum((u + 1) * ratio, nblk * ratio - 1), kv_blk)]


def _window_sink_attention(main, proj, bias, sink):
    B, S, _ = main.shape
    tq = min(BAND_ROWS, S)
    nblk = S // tq
    lead_of = lambda b, h, u: (b,)
    in_specs = _band_specs((None,), lead_of, tq, nblk,
                           lambda b, h, u: M_QA // W_QG + h, lambda b, h, u: M_KVA // W_KV + h)
    in_specs += [pl.BlockSpec((1, GQA * BLOCK, BLOCK + 2 * WIN_A), lambda b, h, u: (h, 0, 0)),
                 pl.BlockSpec((1, GQA * BLOCK, LANES), lambda b, h, u: (h, 0, 0)),
                 pl.BlockSpec((None, tq, W_QG), lambda b, h, u: (b, u, OFF_GA // W_QG + h))]
    return pl.pallas_call(
        functools.partial(_band_kernel, nsub=tq // BLOCK, blk_axis=2, with_sink=True, halo=WIN_A, classes=None),
        grid=(B, KV_A, nblk),
        in_specs=in_specs,
        out_specs=pl.BlockSpec((None, tq, W_QG), lambda b, h, u: (b, u, h)),
        out_shape=jax.ShapeDtypeStruct((B, S, H_A * HEAD_DIM), BF16),
        compiler_params=pltpu.CompilerParams(dimension_semantics=("parallel",) * 3, vmem_limit_bytes=VMEM_LIMIT),
        name="mixer_a",
    )(main, main, main, main, bias, sink, proj)


def _dilated_group_attention(x, bias, group):
    if group == 0:
        B, sub, _ = x.shape
        tq = min(BAND_ROWS, sub)
        d, classes, lead_block = 1, None, (None,)
        lead_of = lambda b, r, j, u: (b,)
        q0, kv0 = M_QC0 // W_QG, M_KVC0 // W_KV
        out_shape = jax.ShapeDtypeStruct((B, sub, W_C), F32)
    else:
        B, d, sub, _ = x.shape
        tq = min(BAND_ROWS, sub)
        classes = min(d, BAND_ROWS // tq)
        lead_block = (None, classes)
        lead_of = lambda b, r, j, u: (b, r)
        q0, kv0 = 0, W_C // W_KV
        out_shape = jax.ShapeDtypeStruct((B, d, sub, W_C), F32)
    nblk = sub // tq
    in_specs = _band_specs(lead_block, lead_of, tq, nblk, lambda b, r, j, u: q0 + j, lambda b, r, j, u: kv0 + j)
    in_specs += [pl.BlockSpec((1, GQA * BLOCK, BLOCK + 2 * C_RADIUS), lambda b, r, j, u: (j, 0, 0))]
    out_spec = pl.BlockSpec(lead_block + (tq, W_QG), lambda b, r, j, u: lead_of(b, r, j, u) + (u, j))
    return pl.pallas_call(
        functools.partial(_band_kernel, nsub=tq // BLOCK, blk_axis=3, with_sink=False, halo=C_RADIUS,
                          classes=classes),
        grid=(B, d // (classes or 1), KV_C, nblk),
        in_specs=in_specs,
        out_specs=[out_spec, out_spec],
        out_shape=[out_shape, out_shape],
        compiler_params=pltpu.CompilerParams(dimension_semantics=("parallel",) * 4, vmem_limit_bytes=VMEM_LIMIT),
        name=f"mixer_c{group}",
    )(x, x, x, x, bias)


def _stage_residue_classes(o1_ref, l1_ref, o2_ref, l2_ref, stage_ref, tm):
    nk = W_C // LANES
    for a, (ref, g) in enumerate(((o1_ref, 1), (l1_ref, 1), (o2_ref, 2), (l2_ref, 2))):
        d = C_PATTERNS[g][1]
        for r in range(d):
            for k in range(nk):
                stage_ref[a * nk + k, pl.ds(r, tm // d, stride=d), :] = ref[0, r, :, k * LANES:(k + 1) * LANES]


def _merge_groups_tile(o0_ref, l0_ref, gate_ref, stage_ref, rows):
    nk = W_C // LANES
    tiles = []
    for k in range(nk):
        cols = slice(k * LANES, (k + 1) * LANES)
        o0, l0 = o0_ref[0, rows, cols], l0_ref[0, rows, cols]
        o1, l1 = stage_ref[k, rows, :], stage_ref[nk + k, rows, :]
        o2, l2 = stage_ref[2 * nk + k, rows, :], stage_ref[3 * nk + k, rows, :]
        m = jnp.maximum(jnp.maximum(l0, l1), l2)
        w0, w1, w2 = jnp.exp2(l0 - m), jnp.exp2(l1 - m), jnp.exp2(l2 - m)
        y = (w0 * o0 + w1 * o1 + w2 * o2) / (w0 + w1 + w2)
        tiles.append((y * _silu(gate_ref[0, rows, cols].astype(F32))).astype(BF16))
    return jnp.concatenate(tiles, axis=1)


def _global_kernel(q_ref, kv_ref, gate_ref, o_ref, v_s, *, seq, tk, prep_rows):
    @pl.when(pl.program_id(2) == 0)
    def _stage_values():
        def body(c, carry):
            rows = pl.ds(pl.multiple_of(c * prep_rows, prep_rows), prep_rows)
            v_s[rows, :] = jnp.concatenate([kv_ref[rows, HEAD_DIM:], jnp.ones((prep_rows, HEAD_DIM), BF16)], axis=1)
            return carry

        lax.fori_loop(0, seq // prep_rows, body, 0)

    q4 = _stack_heads(q_ref[...])
    rows4 = q4.shape[0]
    m = jnp.full((rows4, 1), NEG_INF, F32)
    acc = jnp.zeros((rows4, 2 * HEAD_DIM), F32)
    for c in range(seq // tk):
        rows = slice(c * tk, (c + 1) * tk)
        s = _nt_dot(q4, kv_ref[rows, :HEAD_DIM])
        m_new = jnp.maximum(m, jnp.max(s, axis=-1, keepdims=True))
        p = jnp.exp2(s - m_new).astype(BF16)
        acc = jnp.exp2(m - m_new) * acc + jnp.dot(p, v_s[rows, :], preferred_element_type=F32)
        m = m_new
    o = acc[:, :HEAD_DIM] / acc[:, HEAD_DIM:HEAD_DIM + 1]
    g = gate_ref[...].astype(F32)
    o_ref[...] = (_unstack_heads(o) * _silu(g)).astype(o_ref.dtype)


def _global_axial_attention(main, proj, *, tq=512, tk=512):
    B, S, _ = main.shape
    return pl.pallas_call(
        functools.partial(_global_kernel, seq=S, tk=tk, prep_rows=512),
        grid=(B, KV_B, S // tq),
        in_specs=[pl.BlockSpec((None, tq, W_QG), lambda b, h, i: (b, i, M_QB // W_QG + h)),
                  pl.BlockSpec((None, S, W_KV), lambda b, h, i: (b, 0, M_KVB // W_KV + h)),
                  pl.BlockSpec((None, tq, W_QG), lambda b, h, i: (b, i, OFF_GB // W_QG + h))],
        out_specs=pl.BlockSpec((None, tq, W_QG), lambda b, h, i: (b, i, h)),
        out_shape=jax.ShapeDtypeStruct((B, S, H_B * HEAD_DIM), BF16),
        scratch_shapes=[pltpu.VMEM((S, 2 * HEAD_DIM), BF16)],
        compiler_params=pltpu.CompilerParams(dimension_semantics=("parallel", "parallel", "arbitrary"),
                                             vmem_limit_bytes=VMEM_LIMIT),
        name="mixer_b",
    )(main, main, proj)


def _outproj_kernel(x_ref, a_ref, b_ref, o0_ref, l0_ref, o1_ref, l1_ref, o2_ref, l2_ref, gate_ref,
                    wa_ref, wb_ref, wc_ref, o_ref, stage_ref, *, tm):
    _stage_residue_classes(o1_ref, l1_ref, o2_ref, l2_ref, stage_ref, tm)
    piece = tm // OUTPROJ_PIECES
    for h in range(OUTPROJ_PIECES):
        rows = slice(h * piece, (h + 1) * piece)
        y = jnp.dot(a_ref[0, rows, :], wa_ref[...], preferred_element_type=F32)
        y += jnp.dot(b_ref[0, rows, :], wb_ref[...], preferred_element_type=F32)
        mix_c = _merge_groups_tile(o0_ref, l0_ref, gate_ref, stage_ref, rows)
        y += jnp.dot(mix_c, wc_ref[...], preferred_element_type=F32)
        o_ref[0, rows, :] = x_ref[0, rows, :] + y


def _outproj(x, mix_a, mix_b, outs, lses, proj, w, *, tm=512):
    B, S, _ = x.shape
    ka, kb, kc = H_A * HEAD_DIM, H_B * HEAD_DIM, W_C
    d1, d2 = C_PATTERNS[1][1], C_PATTERNS[2][1]

    def tok(width, col=0):
        return pl.BlockSpec((1, tm, width), lambda b, i: (b, i, col))

    def weight(rows, blk):
        return pl.BlockSpec((rows, D_MODEL), lambda b, i: (blk, 0), pipeline_mode=pl.Buffered(1))

    res1 = pl.BlockSpec((1, d1, tm // d1, kc), lambda b, i: (b, 0, i, 0))
    res2 = pl.BlockSpec((1, d2, tm // d2, kc), lambda b, i: (b, 0, i, 0))
    return pl.pallas_call(
        functools.partial(_outproj_kernel, tm=tm),
        grid=(B, S // tm),
        in_specs=[tok(D_MODEL), tok(ka), tok(kb), tok(kc), tok(kc), res1, res1, res2, res2, tok(kc, OFF_GC // kc),
                  weight(ka, 0), weight(kb, 1), weight(kc, (ka + kb) // kc)],
        out_specs=tok(D_MODEL),
        out_shape=jax.ShapeDtypeStruct((B, S, D_MODEL), F32),
        scratch_shapes=[pltpu.VMEM((4 * (kc // LANES), tm, LANES), F32)],
        compiler_params=pltpu.CompilerParams(dimension_semantics=("parallel", "parallel"),
                                             vmem_limit_bytes=VMEM_LIMIT_OUTPROJ),
        name="outproj",
    )(x, mix_a, mix_b, outs[0], lses[0], outs[1], lses[1], outs[2], lses[2], proj, w, w, w)


def _rope_table(S):
    rows = S // GRID_W
    row = jnp.repeat(jnp.arange(rows, dtype=jnp.int32), GRID_W)
    col = jnp.arange(S, dtype=jnp.int32) % GRID_W
    n_freq = HEAD_DIM // 4
    inv_freq = ROPE_THETA ** (-jnp.arange(n_freq, dtype=F32) / n_freq)
    ang_row = row.astype(F32)[:, None] * inv_freq[None, :]
    ang_col = col.astype(F32)[:, None] * inv_freq[None, :]
    cr, sr, cc, sc = jnp.cos(ang_row), jnp.sin(ang_row), jnp.cos(ang_col), jnp.sin(ang_col)
    cos = jnp.concatenate([cr, cr, cc, cc], axis=1)
    sin_signed = jnp.concatenate([-sr, sr, -sc, sc], axis=1)
    one, zero = jnp.ones_like(cos), jnp.zeros_like(cos)
    return jnp.concatenate([cos, cos, sin_signed, sin_signed, cos, one, sin_signed, zero], axis=1)


def _trunk(x, p, rope):
    B, S, _ = x.shape
    assert rope.shape[0] >= S
    for l in range(DEPTH):
        proj = _inproj(x.reshape(B * S, D_MODEL), p["ln_g"][l], p["w_in"][l]).reshape(B, S, D_IN_PROJ)
        main, xc1, xc2 = _prepare(proj, p["wvec"][l], rope, p["seg_ones"])
        mix_a = _window_sink_attention(main, proj, p["bias_a"], p["sink_a"][l])
        mix_b = _global_axial_attention(main, proj)
        outs, lses = zip(*[_dilated_group_attention(xg, p["bias_c"][g], g) for g, xg in enumerate((main, xc1, xc2))])
        x = _outproj(x, mix_a, mix_b, outs, lses, proj, p["w_out"][l])
    return x


def _prepare_params(ln_g, w_in, q_norm_a, k_norm_a, sink_a, q_norm_b, k_norm_b, q_norm_c, k_norm_c, rel_bias, w_out):
    heads_c = [list(range(H_A + g * H_C, H_A + (g + 1) * H_C)) for g in range(N_C_GROUPS)]
    one = jnp.ones((DEPTH, HEAD_DIM), F32)

    def q_cols(w, n):
        return jnp.tile(w.astype(F32) * Q_SCALE, (1, n))

    def kv_cols(w, n):
        return jnp.tile(jnp.concatenate([w.astype(F32), one], axis=1), (1, n))

    wvec = jnp.concatenate([q_cols(q_norm_a, H_A), q_cols(q_norm_b, H_B), q_cols(q_norm_c, N_C_GROUPS * H_C),
                            kv_cols(k_norm_a, KV_A), kv_cols(k_norm_b, KV_B), kv_cols(k_norm_c, N_C_GROUPS * KV_C)],
                           axis=1)
    seg = np.arange(LANES) // HEAD_DIM
    return {
        "ln_g": ln_g.reshape(DEPTH, 1, D_MODEL),
        "w_in": _permute_columns(w_in.astype(BF16), _proj_column_order()),
        "w_out": w_out.astype(BF16),
        "wvec": wvec.reshape(DEPTH, 1, W_QKV),
        "seg_ones": jnp.asarray((seg[:, None] == seg[None, :]) / HEAD_DIM, BF16),
        "sink_a": jnp.broadcast_to(jnp.repeat(sink_a.astype(F32) * LOG2E, BLOCK, axis=1)[:, :, None],
                                   (DEPTH, H_A * BLOCK, LANES)).reshape(DEPTH, KV_A, GQA * BLOCK, LANES),
        "bias_a": _band_bias(rel_bias, list(range(H_A)), 1, WIN_A, WIN_A),
        "bias_c": [_band_bias(rel_bias, heads_c[g], d, C_RADIUS, C_RADIUS) for g, (w, d) in enumerate(C_PATTERNS)],
    }


def kernel(x_prompt, x_sample, ln_g, w_in, q_norm_a, k_norm_a, sink_a, q_norm_b, k_norm_b,
           q_norm_c, k_norm_c, rel_bias, w_out):
    p = _prepare_params(ln_g, w_in, q_norm_a, k_norm_a, sink_a, q_norm_b, k_norm_b, q_norm_c, k_norm_c,
                        rel_bias, w_out)
    rope = _rope_table(max(x_prompt.shape[1], x_sample.shape[1]))
    return _trunk(x_prompt, p, rope), _trunk(x_sample, p, rope)
```

```python
import functools
import math

import numpy as np
import jax
import jax.numpy as jnp
from jax import lax
from jax.experimental import pallas as pl
from jax.experimental.pallas import tpu as pltpu

F32 = jnp.float32
BF16 = jnp.bfloat16

D_MODEL = 2048
DEPTH = 2
HEAD_DIM = 64
N_HEADS_TOTAL = D_MODEL // HEAD_DIM
H_C = N_HEADS_TOTAL // 4
H_A = (N_HEADS_TOTAL - H_C) // 2
H_B = N_HEADS_TOTAL - H_C - H_A
KV_A = H_A // 4
KV_B = H_B // 4
KV_C = H_C // 4
GQA = 4
C_PATTERNS = ((128, 1), (512, 4), (2048, 16))
N_C_GROUPS = len(C_PATTERNS)
C_RADIUS = 64
assert all(w // (2 * d) == C_RADIUS for w, d in C_PATTERNS)
BLOCK = 128
LANES = 128
WIN_A = 128
GRID_W = 64
ROPE_THETA = 10000.0
T5_BUCKETS = 32
T5_MAX_DISTANCE = 1024
RMS_EPS = 1e-6
NEG_INF = -1e30
LOG2E = math.log2(math.e)
Q_SCALE = HEAD_DIM ** -0.5 * LOG2E

W_QG = GQA * HEAD_DIM
W_KV = 2 * HEAD_DIM
W_C = H_C * HEAD_DIM

OFF_QA = 0
OFF_QB = OFF_QA + H_A * HEAD_DIM
OFF_QC = OFF_QB + H_B * HEAD_DIM
OFF_KVA = OFF_QC + N_C_GROUPS * W_C
OFF_KVB = OFF_KVA + KV_A * W_KV
OFF_KVC = OFF_KVB + KV_B * W_KV
W_QKV = OFF_KVC + N_C_GROUPS * KV_C * W_KV
OFF_GA = W_QKV
OFF_GB = OFF_GA + H_A * HEAD_DIM
OFF_GC = OFF_GB + H_B * HEAD_DIM
D_IN_PROJ = OFF_GC + W_C

M_QA = 0
M_QB = M_QA + H_A * HEAD_DIM
M_QC0 = M_QB + H_B * HEAD_DIM
M_KVA = M_QC0 + W_C
M_KVB = M_KVA + KV_A * W_KV
M_KVC0 = M_KVB + KV_B * W_KV
W_MAIN = M_KVC0 + KV_C * W_KV
W_XC = W_C + KV_C * W_KV

MIB = 1024 * 1024
VMEM_LIMIT = 48 * MIB
VMEM_LIMIT_OUTPROJ = 54 * MIB
VMEM_LIMIT_INPROJ = 56 * MIB
OUTPROJ_PIECES = 2
BAND_ROWS = 2048


def _proj_column_order():
    heads = (H_A, KV_A, KV_A, H_A, H_B, KV_B, KV_B, H_B,
             N_C_GROUPS * H_C, N_C_GROUPS * KV_C, N_C_GROUPS * KV_C, H_C)
    o = np.concatenate([[0], np.cumsum(heads)]) * HEAD_DIM
    qa, ka, va, ga, qb, kb, vb, gb, qc, kc, vc, gc = (np.arange(o[i], o[i + 1]) for i in range(12))

    def kv_pairs(k, v, n):
        head = lambda a, h: a[h * HEAD_DIM:(h + 1) * HEAD_DIM]
        return np.concatenate([np.concatenate([head(k, h), head(v, h)]) for h in range(n)])

    order = np.concatenate([qa, qb, qc, kv_pairs(ka, va, KV_A), kv_pairs(kb, vb, KV_B),
                            kv_pairs(kc, vc, N_C_GROUPS * KV_C), ga, gb, gc])
    assert order.shape == (D_IN_PROJ,) and np.array_equal(np.sort(order), np.arange(D_IN_PROJ))
    return order


def _permute_columns(w, order):
    cuts = np.flatnonzero(np.diff(order) != 1) + 1
    starts = np.concatenate([[0], cuts])
    stops = np.concatenate([cuts, [order.size]])
    return jnp.concatenate([w[..., int(order[a]):int(order[b - 1]) + 1] for a, b in zip(starts, stops)], axis=-1)


def _t5_bucket(rel):
    half = T5_BUCKETS // 2
    max_exact = half // 2
    n = np.abs(rel)
    log_ratio = np.log(np.maximum(n, 1).astype(np.float64) / max_exact) / math.log(T5_MAX_DISTANCE / max_exact)
    large = np.minimum(max_exact + (log_ratio * (half - max_exact)).astype(np.int32), half - 1)
    return np.where(rel > 0, half, 0) + np.where(n < max_exact, n, large)


def _band_bias(rel_bias, heads, dilation, radius, halo):
    width = BLOCK + 2 * halo
    n = width + BLOCK - 1
    rel = np.arange(n) - (halo + BLOCK - 1)
    tab = rel_bias[_t5_bucket(rel * dilation)][:, heads[0]:heads[0] + len(heads)].astype(F32) * LOG2E
    tab = jnp.where((np.abs(rel) <= radius)[:, None], tab, NEG_INF)
    stream = jnp.tile(jnp.pad(tab, ((0, 1), (0, 0))), (BLOCK, 1))[:BLOCK * n]
    b = stream.reshape(BLOCK, n, len(heads))[:, BLOCK - 1:, :]
    return b.transpose(2, 0, 1).reshape(len(heads) // GQA, GQA * BLOCK, width)


def _rms(x, w):
    return x * lax.rsqrt(jnp.mean(x * x, axis=-1, keepdims=True) + RMS_EPS) * w


def _stack_heads(x):
    return jnp.concatenate([x[:, HEAD_DIM * r:HEAD_DIM * (r + 1)] for r in range(GQA)], axis=0)


def _unstack_heads(x):
    rows = x.shape[0] // GQA
    return jnp.concatenate([x[rows * r:rows * (r + 1)] for r in range(GQA)], axis=1)


def _head_rows(x, r):
    return x[r * BLOCK:(r + 1) * BLOCK]


def _silu(g):
    return g * jax.nn.sigmoid(g)


def _nt_dot(a, b):
    return lax.dot_general(a, b, (((1,), (1,)), ((), ())), preferred_element_type=F32)


def _inproj_kernel(x_ref, g_ref, w_ref, o_ref):
    h = _rms(x_ref[...], g_ref[...]).astype(BF16)
    o_ref[...] = jnp.dot(h, w_ref[...], preferred_element_type=F32).astype(o_ref.dtype)


def _inproj(x2, g, w, *, tm=512):
    T = x2.shape[0]
    return pl.pallas_call(
        _inproj_kernel,
        grid=(T // tm,),
        in_specs=[pl.BlockSpec((tm, D_MODEL), lambda i: (i, 0)),
                  pl.BlockSpec((1, D_MODEL), lambda i: (0, 0)),
                  pl.BlockSpec((D_MODEL, D_IN_PROJ), lambda i: (0, 0), pipeline_mode=pl.Buffered(1))],
        out_specs=pl.BlockSpec((tm, D_IN_PROJ), lambda i: (i, 0)),
        out_shape=jax.ShapeDtypeStruct((T, D_IN_PROJ), BF16),
        compiler_params=pltpu.CompilerParams(dimension_semantics=("parallel",),
                                             vmem_limit_bytes=VMEM_LIMIT_INPROJ),
        name="inproj",
    )(x2, g, w)


def _prep_plan():
    plan = []
    for cb in range(W_QKV // LANES):
        col = cb * LANES
        if col < OFF_QB:
            plan.append((False, False, "main", M_QA + col - OFF_QA))
        elif col < OFF_QC:
            plan.append((False, True, "main", M_QB + col - OFF_QB))
        elif col < OFF_KVA:
            g, off = divmod(col - OFF_QC, W_C)
            plan.append((False, False, "main", M_QC0 + off) if g == 0 else (False, False, f"xc{g}", off))
        elif col < OFF_KVB:
            plan.append((True, False, "main", M_KVA + col - OFF_KVA))
        elif col < OFF_KVC:
            plan.append((True, True, "main", M_KVB + col - OFF_KVB))
        else:
            g, off = divmod(col - OFF_KVC, KV_C * W_KV)
            plan.append((True, False, "main", M_KVC0 + off) if g == 0 else (True, False, f"xc{g}", W_C + off))
    return plan


def _prep_kernel(x_ref, w_ref, rope_ref, ones_ref, main_ref, xc1_ref, xc2_ref, stage_ref, *, tp, plan):
    lane = lax.broadcasted_iota(jnp.int32, (tp, LANES), 1)
    is_k = lane < HEAD_DIM
    swap_hi = (lane & 16) != 0
    seg_mean = ones_ref[...]
    n_stage = W_XC // LANES
    for cb, (is_kv, rotary, dest, col) in enumerate(plan):
        cols = slice(cb * LANES, (cb + 1) * LANES)
        x = x_ref[0, :, cols].astype(F32)
        inv = lax.rsqrt(jnp.dot((x * x).astype(BF16), seg_mean, preferred_element_type=F32) + RMS_EPS)
        if is_kv:
            inv = jnp.where(is_k, inv, 1.0)
        y = x * inv * w_ref[:, cols]
        if rotary:
            base = 2 * LANES if is_kv else 0
            cos = rope_ref[:, base:base + LANES]
            sin_signed = rope_ref[:, base + LANES:base + 2 * LANES]
            swapped = jnp.where(swap_hi, pltpu.roll(y, 16, 1), pltpu.roll(y, LANES - 16, 1))
            y = y * cos + swapped * sin_signed
        if dest == "main":
            main_ref[0, :, col:col + LANES] = y.astype(BF16)
        else:
            stage_ref[(int(dest[2]) - 1) * n_stage + col // LANES] = y
    for g, xc_ref in ((1, xc1_ref), (2, xc2_ref)):
        d = C_PATTERNS[g][1]
        for r in range(d):
            for k in range(n_stage):
                piece = stage_ref[(g - 1) * n_stage + k, pl.ds(r, tp // d, stride=d), :]
                xc_ref[0, r, :, k * LANES:(k + 1) * LANES] = piece.astype(BF16)


def _prepare(proj, wvec, rope, seg_ones, *, tp=512):
    B, S, _ = proj.shape
    d1, d2 = C_PATTERNS[1][1], C_PATTERNS[2][1]
    plan = _prep_plan()
    return pl.pallas_call(
        functools.partial(_prep_kernel, tp=tp, plan=plan),
        grid=(B, S // tp),
        in_specs=[pl.BlockSpec((1, tp, W_QKV), lambda b, i: (b, i, 0)),
                  pl.BlockSpec((1, W_QKV), lambda b, i: (0, 0)),
                  pl.BlockSpec((tp, 4 * LANES), lambda b, i: (i, 0)),
                  pl.BlockSpec((LANES, LANES), lambda b, i: (0, 0))],
        out_specs=[pl.BlockSpec((1, tp, W_MAIN), lambda b, i: (b, i, 0)),
                   pl.BlockSpec((1, d1, tp // d1, W_XC), lambda b, i: (b, 0, i, 0)),
                   pl.BlockSpec((1, d2, tp // d2, W_XC), lambda b, i: (b, 0, i, 0))],
        out_shape=[jax.ShapeDtypeStruct((B, S, W_MAIN), BF16),
                   jax.ShapeDtypeStruct((B, d1, S // d1, W_XC), BF16),
                   jax.ShapeDtypeStruct((B, d2, S // d2, W_XC), BF16)],
        scratch_shapes=[pltpu.VMEM((2 * (W_XC // LANES), tp, LANES), F32)],
        compiler_params=pltpu.CompilerParams(dimension_semantics=("parallel", "parallel"),
                                             vmem_limit_bytes=VMEM_LIMIT),
        name="prepare",
    )(proj, wvec, rope, seg_ones)


def _band_kernel(*refs, nsub, blk_axis, with_sink, halo, classes):
    if classes is None:
        _band_rows(*refs, nsub=nsub, blk_axis=blk_axis, with_sink=with_sink, halo=halo)
        return
    q_ref, kp_ref, kc_ref, kn_ref, bias_ref, o_ref, l_ref = refs
    for c in range(classes):
        _band_rows(q_ref.at[c], kp_ref.at[c], kc_ref.at[c], kn_ref.at[c], bias_ref, o_ref.at[c], l_ref.at[c],
                   nsub=nsub, blk_axis=blk_axis, with_sink=with_sink, halo=halo)


def _band_rows(*refs, nsub, blk_axis, with_sink, halo):
    if with_sink:
        q_ref, kp_ref, kc_ref, kn_ref, bias_ref, sink_ref, gate_ref, o_ref = refs
    else:
        q_ref, kp_ref, kc_ref, kn_ref, bias_ref, o_ref, l_ref = refs
    i = pl.program_id(blk_axis)
    nb = pl.num_programs(blk_axis)
    kv = jnp.concatenate([kp_ref[...], kc_ref[...], kn_ref[...]], axis=0)
    bias = bias_ref[0]
    width = BLOCK + 2 * halo
    col = lax.broadcasted_iota(jnp.int32, bias.shape, 1)
    low = lax.broadcasted_iota(jnp.int32, (BLOCK, LANES), 1) < HEAD_DIM
    for t in range(nsub):
        rows = slice(t * BLOCK, (t + 1) * BLOCK)
        q4 = _stack_heads(q_ref[rows, :])
        start = (t + 1) * BLOCK - halo
        win = kv[start:start + width]
        s = _nt_dot(q4, win[:, :HEAD_DIM]) + bias
        if t == 0:
            s = jnp.where((col < halo) & (i == 0), NEG_INF, s)
        if t == nsub - 1:
            s = jnp.where((col >= BLOCK + halo) & (i == nb - 1), NEG_INF, s)
        m = jnp.broadcast_to(jnp.max(s, axis=-1, keepdims=True), (GQA * BLOCK, LANES))
        if with_sink:
            sink = sink_ref[0]
            m = jnp.maximum(m, sink)
        e = jnp.exp2(s - jnp.concatenate([m] * (width // LANES), axis=1)).astype(BF16)
        pv = jnp.dot(e, jnp.concatenate([win, jnp.ones((width, LANES), BF16)], axis=1), preferred_element_type=F32)
        denom = pv[:, LANES:]
        if with_sink:
            denom = denom + jnp.exp2(sink - m)
        o = pv[:, :LANES] / denom
        pairs = [jnp.where(low, pltpu.roll(_head_rows(o, 2 * p), HEAD_DIM, 1), _head_rows(o, 2 * p + 1))
                 for p in range(GQA // 2)]
        o2 = jnp.concatenate(pairs, axis=1)
        if with_sink:
            g = gate_ref[rows, :].astype(F32)
            o_ref[rows, :] = (o2 * _silu(g)).astype(o_ref.dtype)
        else:
            o_ref[rows, :] = o2
            lse = m + jnp.log2(denom)
            l_ref[rows, :] = jnp.concatenate(
                [jnp.where(low, _head_rows(lse, 2 * p), _head_rows(lse, 2 * p + 1)) for p in range(GQA // 2)], axis=1)


def _band_specs(lead_block, lead_of, tq, nblk, q_blk, kv_blk):
    ratio = tq // BLOCK

    def spec(rows, width, row_fn, col_fn):
        return pl.BlockSpec(lead_block + (rows, width), lambda *g: lead_of(*g) + (row_fn(g[-1]), col_fn(*g)))

    return [spec(tq, W_QG, lambda u: u, q_blk),
            spec(BLOCK, W_KV, lambda u: jnp.maximum(u * ratio - 1, 0), kv_blk),
            spec(tq, W_KV, lambda u: u, kv_blk),
            spec(BLOCK, W_KV, lambda u: jnp.minimum((u + 1) * ratio, nblk * ratio - 1), kv_blk)]


def _window_sink_attention(main, proj, bias, sink):
    B, S, _ = main.shape
    tq = min(BAND_ROWS, S)
    nblk = S // tq
    lead_of = lambda b, h, u: (b,)
    in_specs = _band_specs((None,), lead_of, tq, nblk,
                           lambda b, h, u: M_QA // W_QG + h, lambda b, h, u: M_KVA // W_KV + h)
    in_specs += [pl.BlockSpec((1, GQA * BLOCK, BLOCK + 2 * WIN_A), lambda b, h, u: (h, 0, 0)),
                 pl.BlockSpec((1, GQA * BLOCK, LANES), lambda b, h, u: (h, 0, 0)),
                 pl.BlockSpec((None, tq, W_QG), lambda b, h, u: (b, u, OFF_GA // W_QG + h))]
    return pl.pallas_call(
        functools.partial(_band_kernel, nsub=tq // BLOCK, blk_axis=2, with_sink=True, halo=WIN_A, classes=None),
        grid=(B, KV_A, nblk),
        in_specs=in_specs,
        out_specs=pl.BlockSpec((None, tq, W_QG), lambda b, h, u: (b, u, h)),
        out_shape=jax.ShapeDtypeStruct((B, S, H_A * HEAD_DIM), BF16),
        compiler_params=pltpu.CompilerParams(dimension_semantics=("parallel",) * 3, vmem_limit_bytes=VMEM_LIMIT),
        name="mixer_a",
    )(main, main, main, main, bias, sink, proj)


def _dilated_group_attention(x, bias, group):
    if group == 0:
        B, sub, _ = x.shape
        tq = min(BAND_ROWS, sub)
        d, classes, lead_block = 1, None, (None,)
        lead_of = lambda b, r, j, u: (b,)
        q0, kv0 = M_QC0 // W_QG, M_KVC0 // W_KV
        out_shape = jax.ShapeDtypeStruct((B, sub, W_C), F32)
    else:
        B, d, sub, _ = x.shape
        tq = min(BAND_ROWS, sub)
        classes = min(d, BAND_ROWS // tq)
        lead_block = (None, classes)
        lead_of = lambda b, r, j, u: (b, r)
        q0, kv0 = 0, W_C // W_KV
        out_shape = jax.ShapeDtypeStruct((B, d, sub, W_C), F32)
    nblk = sub // tq
    in_specs = _band_specs(lead_block, lead_of, tq, nblk, lambda b, r, j, u: q0 + j, lambda b, r, j, u: kv0 + j)
    in_specs += [pl.BlockSpec((1, GQA * BLOCK, BLOCK + 2 * C_RADIUS), lambda b, r, j, u: (j, 0, 0))]
    out_spec = pl.BlockSpec(lead_block + (tq, W_QG), lambda b, r, j, u: lead_of(b, r, j, u) + (u, j))
    return pl.pallas_call(
        functools.partial(_band_kernel, nsub=tq // BLOCK, blk_axis=3, with_sink=False, halo=C_RADIUS,
                          classes=classes),
        grid=(B, d // (classes or 1), KV_C, nblk),
        in_specs=in_specs,
        out_specs=[out_spec, out_spec],
        out_shape=[out_shape, out_shape],
        compiler_params=pltpu.CompilerParams(dimension_semantics=("parallel",) * 4, vmem_limit_bytes=VMEM_LIMIT),
        name=f"mixer_c{group}",
    )(x, x, x, x, bias)


def _stage_residue_classes(o1_ref, l1_ref, o2_ref, l2_ref, stage_ref, tm):
    nk = W_C // LANES
    for a, (ref, g) in enumerate(((o1_ref, 1), (l1_ref, 1), (o2_ref, 2), (l2_ref, 2))):
        d = C_PATTERNS[g][1]
        for r in range(d):
            for k in range(nk):
                stage_ref[a * nk + k, pl.ds(r, tm // d, stride=d), :] = ref[0, r, :, k * LANES:(k + 1) * LANES]


def _merge_groups_tile(o0_ref, l0_ref, gate_ref, stage_ref, rows):
    nk = W_C // LANES
    tiles = []
    for k in range(nk):
        cols = slice(k * LANES, (k + 1) * LANES)
        o0, l0 = o0_ref[0, rows, cols], l0_ref[0, rows, cols]
        o1, l1 = stage_ref[k, rows, :], stage_ref[nk + k, rows, :]
        o2, l2 = stage_ref[2 * nk + k, rows, :], stage_ref[3 * nk + k, rows, :]
        m = jnp.maximum(jnp.maximum(l0, l1), l2)
        w0, w1, w2 = jnp.exp2(l0 - m), jnp.exp2(l1 - m), jnp.exp2(l2 - m)
        y = (w0 * o0 + w1 * o1 + w2 * o2) / (w0 + w1 + w2)
        tiles.append((y * _silu(gate_ref[0, rows, cols].astype(F32))).astype(BF16))
    return jnp.concatenate(tiles, axis=1)


def _global_kernel(q_ref, kv_ref, gate_ref, o_ref, v_s, *, seq, tk, prep_rows):
    @pl.when(pl.program_id(2) == 0)
    def _stage_values():
        def body(c, carry):
            rows = pl.ds(pl.multiple_of(c * prep_rows, prep_rows), prep_rows)
            v_s[rows, :] = jnp.concatenate([kv_ref[rows, HEAD_DIM:], jnp.ones((prep_rows, HEAD_DIM), BF16)], axis=1)
            return carry

        lax.fori_loop(0, seq // prep_rows, body, 0)

    q4 = _stack_heads(q_ref[...])
    rows4 = q4.shape[0]
    m = jnp.full((rows4, 1), NEG_INF, F32)
    acc = jnp.zeros((rows4, 2 * HEAD_DIM), F32)
    for c in range(seq // tk):
        rows = slice(c * tk, (c + 1) * tk)
        s = _nt_dot(q4, kv_ref[rows, :HEAD_DIM])
        m_new = jnp.maximum(m, jnp.max(s, axis=-1, keepdims=True))
        p = jnp.exp2(s - m_new).astype(BF16)
        acc = jnp.exp2(m - m_new) * acc + jnp.dot(p, v_s[rows, :], preferred_element_type=F32)
        m = m_new
    o = acc[:, :HEAD_DIM] / acc[:, HEAD_DIM:HEAD_DIM + 1]
    g = gate_ref[...].astype(F32)
    o_ref[...] = (_unstack_heads(o) * _silu(g)).astype(o_ref.dtype)


def _global_axial_attention(main, proj, *, tq=512, tk=512):
    B, S, _ = main.shape
    return pl.pallas_call(
        functools.partial(_global_kernel, seq=S, tk=tk, prep_rows=512),
        grid=(B, KV_B, S // tq),
        in_specs=[pl.BlockSpec((None, tq, W_QG), lambda b, h, i: (b, i, M_QB // W_QG + h)),
                  pl.BlockSpec((None, S, W_KV), lambda b, h, i: (b, 0, M_KVB // W_KV + h)),
                  pl.BlockSpec((None, tq, W_QG), lambda b, h, i: (b, i, OFF_GB // W_QG + h))],
        out_specs=pl.BlockSpec((None, tq, W_QG), lambda b, h, i: (b, i, h)),
        out_shape=jax.ShapeDtypeStruct((B, S, H_B * HEAD_DIM), BF16),
        scratch_shapes=[pltpu.VMEM((S, 2 * HEAD_DIM), BF16)],
        compiler_params=pltpu.CompilerParams(dimension_semantics=("parallel", "parallel", "arbitrary"),
                                             vmem_limit_bytes=VMEM_LIMIT),
        name="mixer_b",
    )(main, main, proj)


def _outproj_kernel(x_ref, a_ref, b_ref, o0_ref, l0_ref, o1_ref, l1_ref, o2_ref, l2_ref, gate_ref,
                    wa_ref, wb_ref, wc_ref, o_ref, stage_ref, *, tm):
    _stage_residue_classes(o1_ref, l1_ref, o2_ref, l2_ref, stage_ref, tm)
    piece = tm // OUTPROJ_PIECES
    for h in range(OUTPROJ_PIECES):
        rows = slice(h * piece, (h + 1) * piece)
        y = jnp.dot(a_ref[0, rows, :], wa_ref[...], preferred_element_type=F32)
        y += jnp.dot(b_ref[0, rows, :], wb_ref[...], preferred_element_type=F32)
        mix_c = _merge_groups_tile(o0_ref, l0_ref, gate_ref, stage_ref, rows)
        y += jnp.dot(mix_c, wc_ref[...], preferred_element_type=F32)
        o_ref[0, rows, :] = x_ref[0, rows, :] + y


def _outproj(x, mix_a, mix_b, outs, lses, proj, w, *, tm=512):
    B, S, _ = x.shape
    ka, kb, kc = H_A * HEAD_DIM, H_B * HEAD_DIM, W_C
    d1, d2 = C_PATTERNS[1][1], C_PATTERNS[2][1]

    def tok(width, col=0):
        return pl.BlockSpec((1, tm, width), lambda b, i: (b, i, col))

    def weight(rows, blk):
        return pl.BlockSpec((rows, D_MODEL), lambda b, i: (blk, 0), pipeline_mode=pl.Buffered(1))

    res1 = pl.BlockSpec((1, d1, tm // d1, kc), lambda b, i: (b, 0, i, 0))
    res2 = pl.BlockSpec((1, d2, tm // d2, kc), lambda b, i: (b, 0, i, 0))
    return pl.pallas_call(
        functools.partial(_outproj_kernel, tm=tm),
        grid=(B, S // tm),
        in_specs=[tok(D_MODEL), tok(ka), tok(kb), tok(kc), tok(kc), res1, res1, res2, res2, tok(kc, OFF_GC // kc),
                  weight(ka, 0), weight(kb, 1), weight(kc, (ka + kb) // kc)],
        out_specs=tok(D_MODEL),
        out_shape=jax.ShapeDtypeStruct((B, S, D_MODEL), F32),
        scratch_shapes=[pltpu.VMEM((4 * (kc // LANES), tm, LANES), F32)],
        compiler_params=pltpu.CompilerParams(dimension_semantics=("parallel", "parallel"),
                                             vmem_limit_bytes=VMEM_LIMIT_OUTPROJ),
        name="outproj",
    )(x, mix_a, mix_b, outs[0], lses[0], outs[1], lses[1], outs[2], lses[2], proj, w, w, w)


def _rope_table(S):
    rows = S // GRID_W
    row = jnp.repeat(jnp.arange(rows, dtype=jnp.int32), GRID_W)
    col = jnp.arange(S, dtype=jnp.int32) % GRID_W
    n_freq = HEAD_DIM // 4
    inv_freq = ROPE_THETA ** (-jnp.arange(n_freq, dtype=F32) / n_freq)
    ang_row = row.astype(F32)[:, None] * inv_freq[None, :]
    ang_col = col.astype(F32)[:, None] * inv_freq[None, :]
    cr, sr, cc, sc = jnp.cos(ang_row), jnp.sin(ang_row), jnp.cos(ang_col), jnp.sin(ang_col)
    cos = jnp.concatenate([cr, cr, cc, cc], axis=1)
    sin_signed = jnp.concatenate([-sr, sr, -sc, sc], axis=1)
    one, zero = jnp.ones_like(cos), jnp.zeros_like(cos)
    return jnp.concatenate([cos, cos, sin_signed, sin_signed, cos, one, sin_signed, zero], axis=1)


def _trunk(x, p, rope):
    B, S, _ = x.shape
    assert rope.shape[0] >= S
    for l in range(DEPTH):
        proj = _inproj(x.reshape(B * S, D_MODEL), p["ln_g"][l], p["w_in"][l]).reshape(B, S, D_IN_PROJ)
        main, xc1, xc2 = _prepare(proj, p["wvec"][l], rope, p["seg_ones"])
        mix_a = _window_sink_attention(main, proj, p["bias_a"], p["sink_a"][l])
        mix_b = _global_axial_attention(main, proj)
        outs, lses = zip(*[_dilated_group_attention(xg, p["bias_c"][g], g) for g, xg in enumerate((main, xc1, xc2))])
        x = _outproj(x, mix_a, mix_b, outs, lses, proj, p["w_out"][l])
    return x


def _prepare_params(ln_g, w_in, q_norm_a, k_norm_a, sink_a, q_norm_b, k_norm_b, q_norm_c, k_norm_c, rel_bias, w_out):
    heads_c = [list(range(H_A + g * H_C, H_A + (g + 1) * H_C)) for g in range(N_C_GROUPS)]
    one = jnp.ones((DEPTH, HEAD_DIM), F32)

    def q_cols(w, n):
        return jnp.tile(w.astype(F32) * Q_SCALE, (1, n))

    def kv_cols(w, n):
        return jnp.tile(jnp.concatenate([w.astype(F32), one], axis=1), (1, n))

    wvec = jnp.concatenate([q_cols(q_norm_a, H_A), q_cols(q_norm_b, H_B), q_cols(q_norm_c, N_C_GROUPS * H_C),
                            kv_cols(k_norm_a, KV_A), kv_cols(k_norm_b, KV_B), kv_cols(k_norm_c, N_C_GROUPS * KV_C)],
                           axis=1)
    seg = np.arange(LANES) // HEAD_DIM
    return {
        "ln_g": ln_g.reshape(DEPTH, 1, D_MODEL),
        "w_in": _permute_columns(w_in.astype(BF16), _proj_column_order()),
        "w_out": w_out.astype(BF16),
        "wvec": wvec.reshape(DEPTH, 1, W_QKV),
        "seg_ones": jnp.asarray((seg[:, None] == seg[None, :]) / HEAD_DIM, BF16),
        "sink_a": jnp.broadcast_to(jnp.repeat(sink_a.astype(F32) * LOG2E, BLOCK, axis=1)[:, :, None],
                                   (DEPTH, H_A * BLOCK, LANES)).reshape(DEPTH, KV_A, GQA * BLOCK, LANES),
        "bias_a": _band_bias(rel_bias, list(range(H_A)), 1, WIN_A, WIN_A),
        "bias_c": [_band_bias(rel_bias, heads_c[g], d, C_RADIUS, C_RADIUS) for g, (w, d) in enumerate(C_PATTERNS)],
    }


def kernel(x_prompt, x_sample, ln_g, w_in, q_norm_a, k_norm_a, sink_a, q_norm_b, k_norm_b,
           q_norm_c, k_norm_c, rel_bias, w_out):
    p = _prepare_params(ln_g, w_in, q_norm_a, k_norm_a, sink_a, q_norm_b, k_norm_b, q_norm_c, k_norm_c,
                        rel_bias, w_out)
    rope = _rope_table(max(x_prompt.shape[1], x_sample.shape[1]))
    return _trunk(x_prompt, p, rope), _trunk(x_sample, p, rope)
```

```python
import functools
import math

import numpy as np
import jax
import jax.numpy as jnp
from jax import lax
from jax.experimental import pallas as pl
from jax.experimental.pallas import tpu as pltpu

F32 = jnp.float32
BF16 = jnp.bfloat16

D_MODEL = 2048
DEPTH = 2
HEAD_DIM = 64
N_HEADS_TOTAL = D_MODEL // HEAD_DIM
H_C = N_HEADS_TOTAL // 4
H_A = (N_HEADS_TOTAL - H_C) // 2
H_B = N_HEADS_TOTAL - H_C - H_A
KV_A = H_A // 4
KV_B = H_B // 4
KV_C = H_C // 4
GQA = 4
C_PATTERNS = ((128, 1), (512, 4), (2048, 16))
N_C_GROUPS = len(C_PATTERNS)
C_RADIUS = 64
assert all(w // (2 * d) == C_RADIUS for w, d in C_PATTERNS)
BLOCK = 128
LANES = 128
WIN_A = 128
GRID_W = 64
ROPE_THETA = 10000.0
T5_BUCKETS = 32
T5_MAX_DISTANCE = 1024
RMS_EPS = 1e-6
NEG_INF = -1e30
LOG2E = math.log2(math.e)
Q_SCALE = HEAD_DIM ** -0.5 * LOG2E

W_QG = GQA * HEAD_DIM
W_KV = 2 * HEAD_DIM
W_C = H_C * HEAD_DIM

OFF_QA = 0
OFF_QB = OFF_QA + H_A * HEAD_DIM
OFF_QC = OFF_QB + H_B * HEAD_DIM
OFF_KVA = OFF_QC + N_C_GROUPS * W_C
OFF_KVB = OFF_KVA + KV_A * W_KV
OFF_KVC = OFF_KVB + KV_B * W_KV
W_QKV = OFF_KVC + N_C_GROUPS * KV_C * W_KV
OFF_GA = W_QKV
OFF_GB = OFF_GA + H_A * HEAD_DIM
OFF_GC = OFF_GB + H_B * HEAD_DIM
D_IN_PROJ = OFF_GC + W_C

M_QA = 0
M_QB = M_QA + H_A * HEAD_DIM
M_QC0 = M_QB + H_B * HEAD_DIM
M_KVA = M_QC0 + W_C
M_KVB = M_KVA + KV_A * W_KV
M_KVC0 = M_KVB + KV_B * W_KV
W_MAIN = M_KVC0 + KV_C * W_KV
W_XC = W_C + KV_C * W_KV

MIB = 1024 * 1024
VMEM_LIMIT = 48 * MIB
VMEM_LIMIT_OUTPROJ = 54 * MIB
VMEM_LIMIT_INPROJ = 56 * MIB
OUTPROJ_PIECES = 2
BAND_ROWS = 1024


def _proj_column_order():
    heads = (H_A, KV_A, KV_A, H_A, H_B, KV_B, KV_B, H_B,
             N_C_GROUPS * H_C, N_C_GROUPS * KV_C, N_C_GROUPS * KV_C, H_C)
    o = np.concatenate([[0], np.cumsum(heads)]) * HEAD_DIM
    qa, ka, va, ga, qb, kb, vb, gb, qc, kc, vc, gc = (np.arange(o[i], o[i + 1]) for i in range(12))

    def kv_pairs(k, v, n):
        head = lambda a, h: a[h * HEAD_DIM:(h + 1) * HEAD_DIM]
        return np.concatenate([np.concatenate([head(k, h), head(v, h)]) for h in range(n)])

    order = np.concatenate([qa, qb, qc, kv_pairs(ka, va, KV_A), kv_pairs(kb, vb, KV_B),
                            kv_pairs(kc, vc, N_C_GROUPS * KV_C), ga, gb, gc])
    assert order.shape == (D_IN_PROJ,) and np.array_equal(np.sort(order), np.arange(D_IN_PROJ))
    return order


def _permute_columns(w, order):
    cuts = np.flatnonzero(np.diff(order) != 1) + 1
    starts = np.concatenate([[0], cuts])
    stops = np.concatenate([cuts, [order.size]])
    return jnp.concatenate([w[..., int(order[a]):int(order[b - 1]) + 1] for a, b in zip(starts, stops)], axis=-1)


def _t5_bucket(rel):
    half = T5_BUCKETS // 2
    max_exact = half // 2
    n = np.abs(rel)
    log_ratio = np.log(np.maximum(n, 1).astype(np.float64) / max_exact) / math.log(T5_MAX_DISTANCE / max_exact)
    large = np.minimum(max_exact + (log_ratio * (half - max_exact)).astype(np.int32), half - 1)
    return np.where(rel > 0, half, 0) + np.where(n < max_exact, n, large)


def _band_bias(rel_bias, heads, dilation, radius, halo):
    width = BLOCK + 2 * halo
    n = width + BLOCK - 1
    rel = np.arange(n) - (halo + BLOCK - 1)
    tab = rel_bias[_t5_bucket(rel * dilation)][:, heads[0]:heads[0] + len(heads)].astype(F32) * LOG2E
    tab = jnp.where((np.abs(rel) <= radius)[:, None], tab, NEG_INF)
    stream = jnp.tile(jnp.pad(tab, ((0, 1), (0, 0))), (BLOCK, 1))[:BLOCK * n]
    b = stream.reshape(BLOCK, n, len(heads))[:, BLOCK - 1:, :]
    return b.transpose(2, 0, 1).reshape(len(heads) // GQA, GQA * BLOCK, width)


def _rms(x, w):
    return x * lax.rsqrt(jnp.mean(x * x, axis=-1, keepdims=True) + RMS_EPS) * w


def _stack_heads(x):
    return jnp.concatenate([x[:, HEAD_DIM * r:HEAD_DIM * (r + 1)] for r in range(GQA)], axis=0)


def _unstack_heads(x):
    rows = x.shape[0] // GQA
    return jnp.concatenate([x[rows * r:rows * (r + 1)] for r in range(GQA)], axis=1)


def _head_rows(x, r):
    return x[r * BLOCK:(r + 1) * BLOCK]


def _silu(g):
    return g * jax.nn.sigmoid(g)


def _nt_dot(a, b):
    return lax.dot_general(a, b, (((1,), (1,)), ((), ())), preferred_element_type=F32)


def _inproj_kernel(x_ref, g_ref, w_ref, o_ref):
    h = _rms(x_ref[...], g_ref[...]).astype(BF16)
    o_ref[...] = jnp.dot(h, w_ref[...], preferred_element_type=F32).astype(o_ref.dtype)


def _inproj(x2, g, w, *, tm=512):
    T = x2.shape[0]
    return pl.pallas_call(
        _inproj_kernel,
        grid=(T // tm,),
        in_specs=[pl.BlockSpec((tm, D_MODEL), lambda i: (i, 0)),
                  pl.BlockSpec((1, D_MODEL), lambda i: (0, 0)),
                  pl.BlockSpec((D_MODEL, D_IN_PROJ), lambda i: (0, 0), pipeline_mode=pl.Buffered(1))],
        out_specs=pl.BlockSpec((tm, D_IN_PROJ), lambda i: (i, 0)),
        out_shape=jax.ShapeDtypeStruct((T, D_IN_PROJ), BF16),
        compiler_params=pltpu.CompilerParams(dimension_semantics=("parallel",),
                                             vmem_limit_bytes=VMEM_LIMIT_INPROJ),
        name="inproj",
    )(x2, g, w)


def _prep_plan():
    plan = []
    for cb in range(W_QKV // LANES):
        col = cb * LANES
        if col < OFF_QB:
            plan.append((False, False, "main", M_QA + col - OFF_QA))
        elif col < OFF_QC:
            plan.append((False, True, "main", M_QB + col - OFF_QB))
        elif col < OFF_KVA:
            g, off = divmod(col - OFF_QC, W_C)
            plan.append((False, False, "main", M_QC0 + off) if g == 0 else (False, False, f"xc{g}", off))
        elif col < OFF_KVB:
            plan.append((True, False, "main", M_KVA + col - OFF_KVA))
        elif col < OFF_KVC:
            plan.append((True, True, "main", M_KVB + col - OFF_KVB))
        else:
            g, off = divmod(col - OFF_KVC, KV_C * W_KV)
            plan.append((True, False, "main", M_KVC0 + off) if g == 0 else (True, False, f"xc{g}", W_C + off))
    return plan


def _prep_kernel(x_ref, w_ref, rope_ref, ones_ref, main_ref, xc1_ref, xc2_ref, stage_ref, *, tp, plan):
    lane = lax.broadcasted_iota(jnp.int32, (tp, LANES), 1)
    is_k = lane < HEAD_DIM
    swap_hi = (lane & 16) != 0
    seg_mean = ones_ref[...]
    n_stage = W_XC // LANES
    for cb, (is_kv, rotary, dest, col) in enumerate(plan):
        cols = slice(cb * LANES, (cb + 1) * LANES)
        x = x_ref[0, :, cols].astype(F32)
        inv = lax.rsqrt(jnp.dot((x * x).astype(BF16), seg_mean, preferred_element_type=F32) + RMS_EPS)
        if is_kv:
            inv = jnp.where(is_k, inv, 1.0)
        y = x * inv * w_ref[:, cols]
        if rotary:
            base = 2 * LANES if is_kv else 0
            cos = rope_ref[:, base:base + LANES]
            sin_signed = rope_ref[:, base + LANES:base + 2 * LANES]
            swapped = jnp.where(swap_hi, pltpu.roll(y, 16, 1), pltpu.roll(y, LANES - 16, 1))
            y = y * cos + swapped * sin_signed
        if dest == "main":
            main_ref[0, :, col:col + LANES] = y.astype(BF16)
        else:
            stage_ref[(int(dest[2]) - 1) * n_stage + col // LANES] = y
    for g, xc_ref in ((1, xc1_ref), (2, xc2_ref)):
        d = C_PATTERNS[g][1]
        for r in range(d):
            for k in range(n_stage):
                piece = stage_ref[(g - 1) * n_stage + k, pl.ds(r, tp // d, stride=d), :]
                xc_ref[0, r, :, k * LANES:(k + 1) * LANES] = piece.astype(BF16)


def _prepare(proj, wvec, rope, seg_ones, *, tp=512):
    B, S, _ = proj.shape
    d1, d2 = C_PATTERNS[1][1], C_PATTERNS[2][1]
    plan = _prep_plan()
    return pl.pallas_call(
        functools.partial(_prep_kernel, tp=tp, plan=plan),
        grid=(B, S // tp),
        in_specs=[pl.BlockSpec((1, tp, W_QKV), lambda b, i: (b, i, 0)),
                  pl.BlockSpec((1, W_QKV), lambda b, i: (0, 0)),
                  pl.BlockSpec((tp, 4 * LANES), lambda b, i: (i, 0)),
                  pl.BlockSpec((LANES, LANES), lambda b, i: (0, 0))],
        out_specs=[pl.BlockSpec((1, tp, W_MAIN), lambda b, i: (b, i, 0)),
                   pl.BlockSpec((1, d1, tp // d1, W_XC), lambda b, i: (b, 0, i, 0)),
                   pl.BlockSpec((1, d2, tp // d2, W_XC), lambda b, i: (b, 0, i, 0))],
        out_shape=[jax.ShapeDtypeStruct((B, S, W_MAIN), BF16),
                   jax.ShapeDtypeStruct((B, d1, S // d1, W_XC), BF16),
                   jax.ShapeDtypeStruct((B, d2, S // d2, W_XC), BF16)],
        scratch_shapes=[pltpu.VMEM((2 * (W_XC // LANES), tp, LANES), F32)],
        compiler_params=pltpu.CompilerParams(dimension_semantics=("parallel", "parallel"),
                                             vmem_limit_bytes=VMEM_LIMIT),
        name="prepare",
    )(proj, wvec, rope, seg_ones)


def _band_kernel(*refs, nsub, blk_axis, halo, classes):
    if classes is None:
        _band_rows(*refs, nsub=nsub, blk_axis=blk_axis, halo=halo)
        return
    q_ref, kp_ref, kc_ref, kn_ref, bias_ref, o_ref, l_ref = refs
    for c in range(classes):
        _band_rows(q_ref.at[c], kp_ref.at[c], kc_ref.at[c], kn_ref.at[c], bias_ref, o_ref.at[c], l_ref.at[c],
                   nsub=nsub, blk_axis=blk_axis, halo=halo)


def _band_rows(q_ref, kp_ref, kc_ref, kn_ref, bias_ref, o_ref, l_ref, *, nsub, blk_axis, halo):
    i = pl.program_id(blk_axis)
    nb = pl.num_programs(blk_axis)
    kv = jnp.concatenate([kp_ref[...], kc_ref[...], kn_ref[...]], axis=0)
    bias = bias_ref[0]
    width = BLOCK + 2 * halo
    col = lax.broadcasted_iota(jnp.int32, bias.shape, 1)
    low = lax.broadcasted_iota(jnp.int32, (BLOCK, LANES), 1) < HEAD_DIM
    for t in range(nsub):
        rows = slice(t * BLOCK, (t + 1) * BLOCK)
        q4 = _stack_heads(q_ref[rows, :])
        start = (t + 1) * BLOCK - halo
        win = kv[start:start + width]
        s = _nt_dot(q4, win[:, :HEAD_DIM]) + bias
        if t == 0:
            s = jnp.where((col < halo) & (i == 0), NEG_INF, s)
        if t == nsub - 1:
            s = jnp.where((col >= BLOCK + halo) & (i == nb - 1), NEG_INF, s)
        m = jnp.broadcast_to(jnp.max(s, axis=-1, keepdims=True), (GQA * BLOCK, LANES))
        e = jnp.exp2(s - jnp.concatenate([m] * (width // LANES), axis=1)).astype(BF16)
        pv = jnp.dot(e, jnp.concatenate([win, jnp.ones((width, LANES), BF16)], axis=1), preferred_element_type=F32)
        denom = pv[:, LANES:]
        o = pv[:, :LANES] / denom
        pairs = [jnp.where(low, pltpu.roll(_head_rows(o, 2 * p), HEAD_DIM, 1), _head_rows(o, 2 * p + 1))
                 for p in range(GQA // 2)]
        o_ref[rows, :] = jnp.concatenate(pairs, axis=1)
        lse = m + jnp.log2(denom)
        l_ref[rows, :] = jnp.concatenate(
            [jnp.where(low, _head_rows(lse, 2 * p), _head_rows(lse, 2 * p + 1)) for p in range(GQA // 2)], axis=1)


def _band_specs(lead_block, lead_of, tq, nblk, q_blk, kv_blk):
    ratio = tq // BLOCK

    def spec(rows, width, row_fn, col_fn):
        return pl.BlockSpec(lead_block + (rows, width), lambda *g: lead_of(*g) + (row_fn(g[-1]), col_fn(*g)))

    return [spec(tq, W_QG, lambda u: u, q_blk),
            spec(BLOCK, W_KV, lambda u: jnp.maximum(u * ratio - 1, 0), kv_blk),
            spec(tq, W_KV, lambda u: u, kv_blk),
            spec(BLOCK, W_KV, lambda u: jnp.minimum((u + 1) * ratio, nblk * ratio - 1), kv_blk)]


def _sink_wide_kernel(q_ref, kp_ref, kc_ref, kn_ref, bias_ref, sink_ref, gate_ref, o_ref, *, nsub):
    i = pl.program_id(2)
    nb = pl.num_programs(2)
    width = BLOCK + 2 * WIN_A
    kv = jnp.concatenate([kp_ref[...], kc_ref[...], kn_ref[...]], axis=0)
    bias = bias_ref[0]
    sink = sink_ref[0]
    col = lax.broadcasted_iota(jnp.int32, (BLOCK, width), 1)
    low = lax.broadcasted_iota(jnp.int32, (BLOCK, LANES), 1) < HEAD_DIM
    zero_k = jnp.zeros((HEAD_DIM, width), BF16)
    zero_v = jnp.zeros((width, LANES), BF16)
    zero_half = jnp.zeros((width, HEAD_DIM), BF16)
    for t in range(nsub):
        rows = slice(t * BLOCK, (t + 1) * BLOCK)
        win = kv[t * BLOCK:t * BLOCK + width]
        kt = win[:, :HEAD_DIM].T
        k_diag = jnp.concatenate(
            [jnp.concatenate([kt if c == h else zero_k for c in range(GQA)], axis=1) for h in range(GQA)], axis=0)
        s = jnp.dot(q_ref[rows, :], k_diag, preferred_element_type=F32) + bias
        probs, denoms = [], []
        for h in range(GQA):
            seg = s[:, h * width:(h + 1) * width]
            if t == 0:
                seg = jnp.where((col < WIN_A) & (i == 0), NEG_INF, seg)
            if t == nsub - 1:
                seg = jnp.where((col >= BLOCK + WIN_A) & (i == nb - 1), NEG_INF, seg)
            sink_h = _head_rows(sink, h)
            m = jnp.maximum(jnp.broadcast_to(jnp.max(seg, axis=-1, keepdims=True), (BLOCK, LANES)), sink_h)
            e = jnp.exp2(seg - jnp.concatenate([m] * (width // LANES), axis=1))
            denoms.append(jnp.broadcast_to(jnp.sum(e, axis=-1, keepdims=True), (BLOCK, LANES)) + jnp.exp2(sink_h - m))
            probs.append(e.astype(BF16))
        v_even = jnp.concatenate([win[:, HEAD_DIM:], zero_half], axis=1)
        v_odd = jnp.concatenate([zero_half, win[:, HEAD_DIM:]], axis=1)
        v_diag = jnp.concatenate(
            [jnp.concatenate([(v_odd if h % 2 else v_even) if c == h // 2 else zero_v for c in range(GQA // 2)], axis=1)
             for h in range(GQA)], axis=0)
        pv = jnp.dot(jnp.concatenate(probs, axis=1), v_diag, preferred_element_type=F32)
        denom = jnp.concatenate([jnp.where(low, denoms[2 * p], denoms[2 * p + 1]) for p in range(GQA // 2)], axis=1)
        g = gate_ref[rows, :].astype(F32)
        o_ref[rows, :] = (pv / denom * _silu(g)).astype(o_ref.dtype)


def _window_sink_attention(main, proj, bias, sink):
    B, S, _ = main.shape
    tq = min(BAND_ROWS, S)
    nblk = S // tq
    width = BLOCK + 2 * WIN_A
    bias = bias.reshape(KV_A, GQA, BLOCK, width).transpose(0, 2, 1, 3).reshape(KV_A, BLOCK, GQA * width)
    lead_of = lambda b, h, u: (b,)
    in_specs = _band_specs((None,), lead_of, tq, nblk,
                           lambda b, h, u: M_QA // W_QG + h, lambda b, h, u: M_KVA // W_KV + h)
    in_specs += [pl.BlockSpec((1, BLOCK, GQA * width), lambda b, h, u: (h, 0, 0)),
                 pl.BlockSpec((1, GQA * BLOCK, LANES), lambda b, h, u: (h, 0, 0)),
                 pl.BlockSpec((None, tq, W_QG), lambda b, h, u: (b, u, OFF_GA // W_QG + h))]
    return pl.pallas_call(
        functools.partial(_sink_wide_kernel, nsub=tq // BLOCK),
        grid=(B, KV_A, nblk),
        in_specs=in_specs,
        out_specs=pl.BlockSpec((None, tq, W_QG), lambda b, h, u: (b, u, h)),
        out_shape=jax.ShapeDtypeStruct((B, S, H_A * HEAD_DIM), BF16),
        compiler_params=pltpu.CompilerParams(dimension_semantics=("parallel",) * 3, vmem_limit_bytes=VMEM_LIMIT),
        name="mixer_a",
    )(main, main, main, main, bias, sink, proj)


def _dilated_group_attention(x, bias, group):
    if group == 0:
        B, sub, _ = x.shape
        tq = min(BAND_ROWS, sub)
        d, classes, lead_block = 1, None, (None,)
        lead_of = lambda b, r, j, u: (b,)
        q0, kv0 = M_QC0 // W_QG, M_KVC0 // W_KV
        out_shape = jax.ShapeDtypeStruct((B, sub, W_C), F32)
    else:
        B, d, sub, _ = x.shape
        tq = min(BAND_ROWS, sub)
        classes = min(d, BAND_ROWS // tq)
        lead_block = (None, classes)
        lead_of = lambda b, r, j, u: (b, r)
        q0, kv0 = 0, W_C // W_KV
        out_shape = jax.ShapeDtypeStruct((B, d, sub, W_C), F32)
    nblk = sub // tq
    in_specs = _band_specs(lead_block, lead_of, tq, nblk, lambda b, r, j, u: q0 + j, lambda b, r, j, u: kv0 + j)
    in_specs += [pl.BlockSpec((1, GQA * BLOCK, BLOCK + 2 * C_RADIUS), lambda b, r, j, u: (j, 0, 0))]
    out_spec = pl.BlockSpec(lead_block + (tq, W_QG), lambda b, r, j, u: lead_of(b, r, j, u) + (u, j))
    return pl.pallas_call(
        functools.partial(_band_kernel, nsub=tq // BLOCK, blk_axis=3, halo=C_RADIUS, classes=classes),
        grid=(B, d // (classes or 1), KV_C, nblk),
        in_specs=in_specs,
        out_specs=[out_spec, out_spec],
        out_shape=[out_shape, out_shape],
        compiler_params=pltpu.CompilerParams(dimension_semantics=("parallel",) * 4, vmem_limit_bytes=VMEM_LIMIT),
        name=f"mixer_c{group}",
    )(x, x, x, x, bias)


def _stage_residue_classes(o1_ref, l1_ref, o2_ref, l2_ref, stage_ref, tm):
    nk = W_C // LANES
    for a, (ref, g) in enumerate(((o1_ref, 1), (l1_ref, 1), (o2_ref, 2), (l2_ref, 2))):
        d = C_PATTERNS[g][1]
        for r in range(d):
            for k in range(nk):
                stage_ref[a * nk + k, pl.ds(r, tm // d, stride=d), :] = ref[0, r, :, k * LANES:(k + 1) * LANES]


def _merge_groups_tile(o0_ref, l0_ref, gate_ref, stage_ref, rows):
    nk = W_C // LANES
    tiles = []
    for k in range(nk):
        cols = slice(k * LANES, (k + 1) * LANES)
        o0, l0 = o0_ref[0, rows, cols], l0_ref[0, rows, cols]
        o1, l1 = stage_ref[k, rows, :], stage_ref[nk + k, rows, :]
        o2, l2 = stage_ref[2 * nk + k, rows, :], stage_ref[3 * nk + k, rows, :]
        m = jnp.maximum(jnp.maximum(l0, l1), l2)
        w0, w1, w2 = jnp.exp2(l0 - m), jnp.exp2(l1 - m), jnp.exp2(l2 - m)
        y = (w0 * o0 + w1 * o1 + w2 * o2) / (w0 + w1 + w2)
        tiles.append((y * _silu(gate_ref[0, rows, cols].astype(F32))).astype(BF16))
    return jnp.concatenate(tiles, axis=1)


def _global_kernel(q_ref, kv_ref, gate_ref, o_ref, v_s, *, seq, tk, prep_rows):
    @pl.when(pl.program_id(2) == 0)
    def _stage_values():
        def body(c, carry):
            rows = pl.ds(pl.multiple_of(c * prep_rows, prep_rows), prep_rows)
            v_s[rows, :] = jnp.concatenate([kv_ref[rows, HEAD_DIM:], jnp.ones((prep_rows, HEAD_DIM), BF16)], axis=1)
            return carry

        lax.fori_loop(0, seq // prep_rows, body, 0)

    q4 = _stack_heads(q_ref[...])
    rows4 = q4.shape[0]
    m = jnp.full((rows4, 1), NEG_INF, F32)
    acc = jnp.zeros((rows4, 2 * HEAD_DIM), F32)
    for c in range(seq // tk):
        rows = slice(c * tk, (c + 1) * tk)
        s = _nt_dot(q4, kv_ref[rows, :HEAD_DIM])
        m_new = jnp.maximum(m, jnp.max(s, axis=-1, keepdims=True))
        p = jnp.exp2(s - m_new).astype(BF16)
        acc = jnp.exp2(m - m_new) * acc + jnp.dot(p, v_s[rows, :], preferred_element_type=F32)
        m = m_new
    o = acc[:, :HEAD_DIM] / acc[:, HEAD_DIM:HEAD_DIM + 1]
    g = gate_ref[...].astype(F32)
    o_ref[...] = (_unstack_heads(o) * _silu(g)).astype(o_ref.dtype)


def _global_axial_attention(main, proj, *, tq=512, tk=512):
    B, S, _ = main.shape
    return pl.pallas_call(
        functools.partial(_global_kernel, seq=S, tk=tk, prep_rows=512),
        grid=(B, KV_B, S // tq),
        in_specs=[pl.BlockSpec((None, tq, W_QG), lambda b, h, i: (b, i, M_QB // W_QG + h)),
                  pl.BlockSpec((None, S, W_KV), lambda b, h, i: (b, 0, M_KVB // W_KV + h)),
                  pl.BlockSpec((None, tq, W_QG), lambda b, h, i: (b, i, OFF_GB // W_QG + h))],
        out_specs=pl.BlockSpec((None, tq, W_QG), lambda b, h, i: (b, i, h)),
        out_shape=jax.ShapeDtypeStruct((B, S, H_B * HEAD_DIM), BF16),
        scratch_shapes=[pltpu.VMEM((S, 2 * HEAD_DIM), BF16)],
        compiler_params=pltpu.CompilerParams(dimension_semantics=("parallel", "parallel", "arbitrary"),
                                             vmem_limit_bytes=VMEM_LIMIT),
        name="mixer_b",
    )(main, main, proj)


def _outproj_kernel(x_ref, a_ref, b_ref, o0_ref, l0_ref, o1_ref, l1_ref, o2_ref, l2_ref, gate_ref,
                    wa_ref, wb_ref, wc_ref, o_ref, stage_ref, *, tm):
    _stage_residue_classes(o1_ref, l1_ref, o2_ref, l2_ref, stage_ref, tm)
    piece = tm // OUTPROJ_PIECES
    for h in range(OUTPROJ_PIECES):
        rows = slice(h * piece, (h + 1) * piece)
        y = jnp.dot(a_ref[0, rows, :], wa_ref[...], preferred_element_type=F32)
        y += jnp.dot(b_ref[0, rows, :], wb_ref[...], preferred_element_type=F32)
        mix_c = _merge_groups_tile(o0_ref, l0_ref, gate_ref, stage_ref, rows)
        y += jnp.dot(mix_c, wc_ref[...], preferred_element_type=F32)
        o_ref[0, rows, :] = x_ref[0, rows, :] + y


def _outproj(x, mix_a, mix_b, outs, lses, proj, w, *, tm=512):
    B, S, _ = x.shape
    ka, kb, kc = H_A * HEAD_DIM, H_B * HEAD_DIM, W_C
    d1, d2 = C_PATTERNS[1][1], C_PATTERNS[2][1]

    def tok(width, col=0):
        return pl.BlockSpec((1, tm, width), lambda b, i: (b, i, col))

    def weight(rows, blk):
        return pl.BlockSpec((rows, D_MODEL), lambda b, i: (blk, 0), pipeline_mode=pl.Buffered(1))

    res1 = pl.BlockSpec((1, d1, tm // d1, kc), lambda b, i: (b, 0, i, 0))
    res2 = pl.BlockSpec((1, d2, tm // d2, kc), lambda b, i: (b, 0, i, 0))
    return pl.pallas_call(
        functools.partial(_outproj_kernel, tm=tm),
        grid=(B, S // tm),
        in_specs=[tok(D_MODEL), tok(ka), tok(kb), tok(kc), tok(kc), res1, res1, res2, res2, tok(kc, OFF_GC // kc),
                  weight(ka, 0), weight(kb, 1), weight(kc, (ka + kb) // kc)],
        out_specs=tok(D_MODEL),
        out_shape=jax.ShapeDtypeStruct((B, S, D_MODEL), F32),
        scratch_shapes=[pltpu.VMEM((4 * (kc // LANES), tm, LANES), F32)],
        compiler_params=pltpu.CompilerParams(dimension_semantics=("parallel", "parallel"),
                                             vmem_limit_bytes=VMEM_LIMIT_OUTPROJ),
        name="outproj",
    )(x, mix_a, mix_b, outs[0], lses[0], outs[1], lses[1], outs[2], lses[2], proj, w, w, w)


def _rope_table(S):
    rows = S // GRID_W
    row = jnp.repeat(jnp.arange(rows, dtype=jnp.int32), GRID_W)
    col = jnp.arange(S, dtype=jnp.int32) % GRID_W
    n_freq = HEAD_DIM // 4
    inv_freq = ROPE_THETA ** (-jnp.arange(n_freq, dtype=F32) / n_freq)
    ang_row = row.astype(F32)[:, None] * inv_freq[None, :]
    ang_col = col.astype(F32)[:, None] * inv_freq[None, :]
    cr, sr, cc, sc = jnp.cos(ang_row), jnp.sin(ang_row), jnp.cos(ang_col), jnp.sin(ang_col)
    cos = jnp.concatenate([cr, cr, cc, cc], axis=1)
    sin_signed = jnp.concatenate([-sr, sr, -sc, sc], axis=1)
    one, zero = jnp.ones_like(cos), jnp.zeros_like(cos)
    return jnp.concatenate([cos, cos, sin_signed, sin_signed, cos, one, sin_signed, zero], axis=1)


def _trunk(x, p, rope):
    B, S, _ = x.shape
    assert rope.shape[0] >= S
    for l in range(DEPTH):
        proj = _inproj(x.reshape(B * S, D_MODEL), p["ln_g"][l], p["w_in"][l]).reshape(B, S, D_IN_PROJ)
        main, xc1, xc2 = _prepare(proj, p["wvec"][l], rope, p["seg_ones"])
        mix_a = _window_sink_attention(main, proj, p["bias_a"], p["sink_a"][l])
        mix_b = _global_axial_attention(main, proj)
        outs, lses = zip(*[_dilated_group_attention(xg, p["bias_c"][g], g) for g, xg in enumerate((main, xc1, xc2))])
        x = _outproj(x, mix_a, mix_b, outs, lses, proj, p["w_out"][l])
    return x


def _prepare_params(ln_g, w_in, q_norm_a, k_norm_a, sink_a, q_norm_b, k_norm_b, q_norm_c, k_norm_c, rel_bias, w_out):
    heads_c = [list(range(H_A + g * H_C, H_A + (g + 1) * H_C)) for g in range(N_C_GROUPS)]
    one = jnp.ones((DEPTH, HEAD_DIM), F32)

    def q_cols(w, n):
        return jnp.tile(w.astype(F32) * Q_SCALE, (1, n))

    def kv_cols(w, n):
        return jnp.tile(jnp.concatenate([w.astype(F32), one], axis=1), (1, n))

    wvec = jnp.concatenate([q_cols(q_norm_a, H_A), q_cols(q_norm_b, H_B), q_cols(q_norm_c, N_C_GROUPS * H_C),
                            kv_cols(k_norm_a, KV_A), kv_cols(k_norm_b, KV_B), kv_cols(k_norm_c, N_C_GROUPS * KV_C)],
                           axis=1)
    seg = np.arange(LANES) // HEAD_DIM
    return {
        "ln_g": ln_g.reshape(DEPTH, 1, D_MODEL),
        "w_in": _permute_columns(w_in.astype(BF16), _proj_column_order()),
        "w_out": w_out.astype(BF16),
        "wvec": wvec.reshape(DEPTH, 1, W_QKV),
        "seg_ones": jnp.asarray((seg[:, None] == seg[None, :]) / HEAD_DIM, BF16),
        "sink_a": jnp.broadcast_to(jnp.repeat(sink_a.astype(F32) * LOG2E, BLOCK, axis=1)[:, :, None],
                                   (DEPTH, H_A * BLOCK, LANES)).reshape(DEPTH, KV_A, GQA * BLOCK, LANES),
        "bias_a": _band_bias(rel_bias, list(range(H_A)), 1, WIN_A, WIN_A),
        "bias_c": [_band_bias(rel_bias, heads_c[g], d, C_RADIUS, C_RADIUS) for g, (w, d) in enumerate(C_PATTERNS)],
    }


def kernel(x_prompt, x_sample, ln_g, w_in, q_norm_a, k_norm_a, sink_a, q_norm_b, k_norm_b,
           q_norm_c, k_norm_c, rel_bias, w_out):
    p = _prepare_params(ln_g, w_in, q_norm_a, k_norm_a, sink_a, q_norm_b, k_norm_b, q_norm_c, k_norm_c,
                        rel_bias, w_out)
    rope = _rope_table(max(x_prompt.shape[1], x_sample.shape[1]))
    return _trunk(x_prompt, p, rope), _trunk(x_sample, p, rope)
```

```python
import functools
import math

import numpy as np
import jax
import jax.numpy as jnp
from jax import lax
from jax.experimental import pallas as pl
from jax.experimental.pallas import tpu as pltpu

F32 = jnp.float32
BF16 = jnp.bfloat16

D_MODEL = 2048
DEPTH = 2
HEAD_DIM = 64
N_HEADS_TOTAL = D_MODEL // HEAD_DIM
H_C = N_HEADS_TOTAL // 4
H_A = (N_HEADS_TOTAL - H_C) // 2
H_B = N_HEADS_TOTAL - H_C - H_A
KV_A = H_A // 4
KV_B = H_B // 4
KV_C = H_C // 4
GQA = 4
C_PATTERNS = ((128, 1), (512, 4), (2048, 16))
N_C_GROUPS = len(C_PATTERNS)
C_RADIUS = 64
assert all(w // (2 * d) == C_RADIUS for w, d in C_PATTERNS)
BLOCK = 128
LANES = 128
WIN_A = 128
GRID_W = 64
ROPE_THETA = 10000.0
T5_BUCKETS = 32
T5_MAX_DISTANCE = 1024
RMS_EPS = 1e-6
NEG_INF = -1e30
LOG2E = math.log2(math.e)
Q_SCALE = HEAD_DIM ** -0.5 * LOG2E

W_QG = GQA * HEAD_DIM
W_KV = 2 * HEAD_DIM
W_C = H_C * HEAD_DIM

OFF_QA = 0
OFF_QB = OFF_QA + H_A * HEAD_DIM
OFF_QC = OFF_QB + H_B * HEAD_DIM
OFF_KVA = OFF_QC + N_C_GROUPS * W_C
OFF_KVB = OFF_KVA + KV_A * W_KV
OFF_KVC = OFF_KVB + KV_B * W_KV
W_QKV = OFF_KVC + N_C_GROUPS * KV_C * W_KV
OFF_GA = W_QKV
OFF_GB = OFF_GA + H_A * HEAD_DIM
OFF_GC = OFF_GB + H_B * HEAD_DIM
D_IN_PROJ = OFF_GC + W_C

M_QA = 0
M_QB = M_QA + H_A * HEAD_DIM
M_QC0 = M_QB + H_B * HEAD_DIM
M_KVA = M_QC0 + W_C
M_KVB = M_KVA + KV_A * W_KV
M_KVC0 = M_KVB + KV_B * W_KV
W_MAIN = M_KVC0 + KV_C * W_KV
W_XC = W_C + KV_C * W_KV

MIB = 1024 * 1024
VMEM_LIMIT = 48 * MIB
VMEM_LIMIT_OUTPROJ = 54 * MIB
VMEM_LIMIT_INPROJ = 56 * MIB
OUTPROJ_PIECES = 2
BAND_ROWS = 1024


def _proj_column_order():
    heads = (H_A, KV_A, KV_A, H_A, H_B, KV_B, KV_B, H_B,
             N_C_GROUPS * H_C, N_C_GROUPS * KV_C, N_C_GROUPS * KV_C, H_C)
    o = np.concatenate([[0], np.cumsum(heads)]) * HEAD_DIM
    qa, ka, va, ga, qb, kb, vb, gb, qc, kc, vc, gc = (np.arange(o[i], o[i + 1]) for i in range(12))

    def kv_pairs(k, v, n):
        head = lambda a, h: a[h * HEAD_DIM:(h + 1) * HEAD_DIM]
        return np.concatenate([np.concatenate([head(k, h), head(v, h)]) for h in range(n)])

    order = np.concatenate([qa, qb, qc, kv_pairs(ka, va, KV_A), kv_pairs(kb, vb, KV_B),
                            kv_pairs(kc, vc, N_C_GROUPS * KV_C), ga, gb, gc])
    assert order.shape == (D_IN_PROJ,) and np.array_equal(np.sort(order), np.arange(D_IN_PROJ))
    return order


def _permute_columns(w, order):
    cuts = np.flatnonzero(np.diff(order) != 1) + 1
    starts = np.concatenate([[0], cuts])
    stops = np.concatenate([cuts, [order.size]])
    return jnp.concatenate([w[..., int(order[a]):int(order[b - 1]) + 1] for a, b in zip(starts, stops)], axis=-1)


def _t5_bucket(rel):
    half = T5_BUCKETS // 2
    max_exact = half // 2
    n = np.abs(rel)
    log_ratio = np.log(np.maximum(n, 1).astype(np.float64) / max_exact) / math.log(T5_MAX_DISTANCE / max_exact)
    large = np.minimum(max_exact + (log_ratio * (half - max_exact)).astype(np.int32), half - 1)
    return np.where(rel > 0, half, 0) + np.where(n < max_exact, n, large)


def _band_bias(rel_bias, heads, dilation, radius, halo):
    width = BLOCK + 2 * halo
    n = width + BLOCK - 1
    rel = np.arange(n) - (halo + BLOCK - 1)
    tab = rel_bias[_t5_bucket(rel * dilation)][:, heads[0]:heads[0] + len(heads)].astype(F32) * LOG2E
    tab = jnp.where((np.abs(rel) <= radius)[:, None], tab, NEG_INF)
    stream = jnp.tile(jnp.pad(tab, ((0, 1), (0, 0))), (BLOCK, 1))[:BLOCK * n]
    b = stream.reshape(BLOCK, n, len(heads))[:, BLOCK - 1:, :]
    return b.transpose(2, 0, 1).reshape(len(heads) // GQA, GQA * BLOCK, width)


def _rms(x, w):
    return x * lax.rsqrt(jnp.mean(x * x, axis=-1, keepdims=True) + RMS_EPS) * w


def _stack_heads(x):
    return jnp.concatenate([x[:, HEAD_DIM * r:HEAD_DIM * (r + 1)] for r in range(GQA)], axis=0)


def _unstack_heads(x):
    rows = x.shape[0] // GQA
    return jnp.concatenate([x[rows * r:rows * (r + 1)] for r in range(GQA)], axis=1)


def _head_rows(x, r):
    return x[r * BLOCK:(r + 1) * BLOCK]


def _silu(g):
    return g * jax.nn.sigmoid(g)


def _nt_dot(a, b):
    return lax.dot_general(a, b, (((1,), (1,)), ((), ())), preferred_element_type=F32)


def _inproj_kernel(x_ref, g_ref, w_ref, o_ref):
    h = _rms(x_ref[...], g_ref[...]).astype(BF16)
    o_ref[...] = jnp.dot(h, w_ref[...], preferred_element_type=F32).astype(o_ref.dtype)


def _inproj(x2, g, w, *, tm=512):
    T = x2.shape[0]
    return pl.pallas_call(
        _inproj_kernel,
        grid=(T // tm,),
        in_specs=[pl.BlockSpec((tm, D_MODEL), lambda i: (i, 0)),
                  pl.BlockSpec((1, D_MODEL), lambda i: (0, 0)),
                  pl.BlockSpec((D_MODEL, D_IN_PROJ), lambda i: (0, 0), pipeline_mode=pl.Buffered(1))],
        out_specs=pl.BlockSpec((tm, D_IN_PROJ), lambda i: (i, 0)),
        out_shape=jax.ShapeDtypeStruct((T, D_IN_PROJ), BF16),
        compiler_params=pltpu.CompilerParams(dimension_semantics=("parallel",),
                                             vmem_limit_bytes=VMEM_LIMIT_INPROJ),
        name="inproj",
    )(x2, g, w)


def _prep_plan():
    plan = []
    for cb in range(W_QKV // LANES):
        col = cb * LANES
        if col < OFF_QB:
            plan.append((False, False, "main", M_QA + col - OFF_QA))
        elif col < OFF_QC:
            plan.append((False, True, "main", M_QB + col - OFF_QB))
        elif col < OFF_KVA:
            g, off = divmod(col - OFF_QC, W_C)
            plan.append((False, False, "main", M_QC0 + off) if g == 0 else (False, False, f"xc{g}", off))
        elif col < OFF_KVB:
            plan.append((True, False, "main", M_KVA + col - OFF_KVA))
        elif col < OFF_KVC:
            plan.append((True, True, "main", M_KVB + col - OFF_KVB))
        else:
            g, off = divmod(col - OFF_KVC, KV_C * W_KV)
            plan.append((True, False, "main", M_KVC0 + off) if g == 0 else (True, False, f"xc{g}", W_C + off))
    return plan


def _prep_kernel(x_ref, w_ref, rope_ref, ones_ref, main_ref, xc1_ref, xc2_ref, stage_ref, *, tp, plan):
    lane = lax.broadcasted_iota(jnp.int32, (tp, LANES), 1)
    is_k = lane < HEAD_DIM
    swap_hi = (lane & 16) != 0
    seg_mean = ones_ref[...]
    n_stage = W_XC // LANES
    for cb, (is_kv, rotary, dest, col) in enumerate(plan):
        cols = slice(cb * LANES, (cb + 1) * LANES)
        x = x_ref[0, :, cols].astype(F32)
        inv = lax.rsqrt(jnp.dot((x * x).astype(BF16), seg_mean, preferred_element_type=F32) + RMS_EPS)
        if is_kv:
            inv = jnp.where(is_k, inv, 1.0)
        y = x * inv * w_ref[:, cols]
        if rotary:
            base = 2 * LANES if is_kv else 0
            cos = rope_ref[:, base:base + LANES]
            sin_signed = rope_ref[:, base + LANES:base + 2 * LANES]
            swapped = jnp.where(swap_hi, pltpu.roll(y, 16, 1), pltpu.roll(y, LANES - 16, 1))
            y = y * cos + swapped * sin_signed
        if dest == "main":
            main_ref[0, :, col:col + LANES] = y.astype(BF16)
        else:
            stage_ref[(int(dest[2]) - 1) * n_stage + col // LANES] = y
    for g, xc_ref in ((1, xc1_ref), (2, xc2_ref)):
        d = C_PATTERNS[g][1]
        for r in range(d):
            for k in range(n_stage):
                piece = stage_ref[(g - 1) * n_stage + k, pl.ds(r, tp // d, stride=d), :]
                xc_ref[0, r, :, k * LANES:(k + 1) * LANES] = piece.astype(BF16)


def _prepare(proj, wvec, rope, seg_ones, *, tp=512):
    B, S, _ = proj.shape
    d1, d2 = C_PATTERNS[1][1], C_PATTERNS[2][1]
    plan = _prep_plan()
    return pl.pallas_call(
        functools.partial(_prep_kernel, tp=tp, plan=plan),
        grid=(B, S // tp),
        in_specs=[pl.BlockSpec((1, tp, W_QKV), lambda b, i: (b, i, 0)),
                  pl.BlockSpec((1, W_QKV), lambda b, i: (0, 0)),
                  pl.BlockSpec((tp, 4 * LANES), lambda b, i: (i, 0)),
                  pl.BlockSpec((LANES, LANES), lambda b, i: (0, 0))],
        out_specs=[pl.BlockSpec((1, tp, W_MAIN), lambda b, i: (b, i, 0)),
                   pl.BlockSpec((1, d1, tp // d1, W_XC), lambda b, i: (b, 0, i, 0)),
                   pl.BlockSpec((1, d2, tp // d2, W_XC), lambda b, i: (b, 0, i, 0))],
        out_shape=[jax.ShapeDtypeStruct((B, S, W_MAIN), BF16),
                   jax.ShapeDtypeStruct((B, d1, S // d1, W_XC), BF16),
                   jax.ShapeDtypeStruct((B, d2, S // d2, W_XC), BF16)],
        scratch_shapes=[pltpu.VMEM((2 * (W_XC // LANES), tp, LANES), F32)],
        compiler_params=pltpu.CompilerParams(dimension_semantics=("parallel", "parallel"),
                                             vmem_limit_bytes=VMEM_LIMIT),
        name="prepare",
    )(proj, wvec, rope, seg_ones)


def _band_kernel(*refs, nsub, blk_axis, with_sink, halo, classes):
    if classes is None:
        _band_rows(*refs, nsub=nsub, blk_axis=blk_axis, with_sink=with_sink, halo=halo)
        return
    q_ref, kp_ref, kc_ref, kn_ref, bias_ref, o_ref, l_ref = refs
    for c in range(classes):
        _band_rows(q_ref.at[c], kp_ref.at[c], kc_ref.at[c], kn_ref.at[c], bias_ref, o_ref.at[c], l_ref.at[c],
                   nsub=nsub, blk_axis=blk_axis, with_sink=with_sink, halo=halo)


def _band_rows(*refs, nsub, blk_axis, with_sink, halo):
    if with_sink:
        q_ref, kp_ref, kc_ref, kn_ref, bias_ref, sink_ref, gate_ref, o_ref = refs
    else:
        q_ref, kp_ref, kc_ref, kn_ref, bias_ref, o_ref, l_ref = refs
    i = pl.program_id(blk_axis)
    nb = pl.num_programs(blk_axis)
    kv = jnp.concatenate([kp_ref[...], kc_ref[...], kn_ref[...]], axis=0)
    bias = bias_ref[0]
    width = BLOCK + 2 * halo
    col = lax.broadcasted_iota(jnp.int32, bias.shape, 1)
    low = lax.broadcasted_iota(jnp.int32, (BLOCK, LANES), 1) < HEAD_DIM
    for t in range(nsub):
        rows = slice(t * BLOCK, (t + 1) * BLOCK)
        q4 = _stack_heads(q_ref[rows, :])
        start = (t + 1) * BLOCK - halo
        win = kv[start:start + width]
        s = _nt_dot(q4, win[:, :HEAD_DIM]) + bias
        if t == 0:
            s = jnp.where((col < halo) & (i == 0), NEG_INF, s)
        if t == nsub - 1:
            s = jnp.where((col >= BLOCK + halo) & (i == nb - 1), NEG_INF, s)
        m = jnp.broadcast_to(jnp.max(s, axis=-1, keepdims=True), (GQA * BLOCK, LANES))
        if with_sink:
            sink = sink_ref[0]
            m = jnp.maximum(m, sink)
        e = jnp.exp2(s - jnp.concatenate([m] * (width // LANES), axis=1)).astype(BF16)
        pv = jnp.dot(e, jnp.concatenate([win, jnp.ones((width, LANES), BF16)], axis=1), preferred_element_type=F32)
        denom = pv[:, LANES:]
        if with_sink:
            denom = denom + jnp.exp2(sink - m)
        o = pv[:, :LANES] / denom
        pairs = [jnp.where(low, pltpu.roll(_head_rows(o, 2 * p), HEAD_DIM, 1), _head_rows(o, 2 * p + 1))
                 for p in range(GQA // 2)]
        o2 = jnp.concatenate(pairs, axis=1)
        if with_sink:
            g = gate_ref[rows, :].astype(F32)
            o_ref[rows, :] = (o2 * _silu(g)).astype(o_ref.dtype)
        else:
            o_ref[rows, :] = o2
            lse = m + jnp.log2(denom)
            l_ref[rows, :] = jnp.concatenate(
                [jnp.where(low, _head_rows(lse, 2 * p), _head_rows(lse, 2 * p + 1)) for p in range(GQA // 2)], axis=1)


def _band_specs(lead_block, lead_of, tq, nblk, q_blk, kv_blk):
    ratio = tq // BLOCK

    def spec(rows, width, row_fn, col_fn):
        return pl.BlockSpec(lead_block + (rows, width), lambda *g: lead_of(*g) + (row_fn(g[-1]), col_fn(*g)))

    return [spec(tq, W_QG, lambda u: u, q_blk),
            spec(BLOCK, W_KV, lambda u: jnp.maximum(u * ratio - 1, 0), kv_blk),
            spec(tq, W_KV, lambda u: u, kv_blk),
            spec(BLOCK, W_KV, lambda u: jnp.minimum((u + 1) * ratio, nblk * ratio - 1), kv_blk)]


def _window_sink_attention(main, proj, bias, sink):
    B, S, _ = main.shape
    tq = min(BAND_ROWS, S)
    nblk = S // tq
    lead_of = lambda b, h, u: (b,)
    in_specs = _band_specs((None,), lead_of, tq, nblk,
                           lambda b, h, u: M_QA // W_QG + h, lambda b, h, u: M_KVA // W_KV + h)
    in_specs += [pl.BlockSpec((1, GQA * BLOCK, BLOCK + 2 * WIN_A), lambda b, h, u: (h, 0, 0)),
                 pl.BlockSpec((1, GQA * BLOCK, LANES), lambda b, h, u: (h, 0, 0)),
                 pl.BlockSpec((None, tq, W_QG), lambda b, h, u: (b, u, OFF_GA // W_QG + h))]
    return pl.pallas_call(
        functools.partial(_band_kernel, nsub=tq // BLOCK, blk_axis=2, with_sink=True, halo=WIN_A, classes=None),
        grid=(B, KV_A, nblk),
        in_specs=in_specs,
        out_specs=pl.BlockSpec((None, tq, W_QG), lambda b, h, u: (b, u, h)),
        out_shape=jax.ShapeDtypeStruct((B, S, H_A * HEAD_DIM), BF16),
        compiler_params=pltpu.CompilerParams(dimension_semantics=("parallel",) * 3, vmem_limit_bytes=VMEM_LIMIT),
        name="mixer_a",
    )(main, main, main, main, bias, sink, proj)


def _dilated_group_attention(x, bias, group):
    if group == 0:
        B, sub, _ = x.shape
        tq = min(BAND_ROWS, sub)
        d, classes, lead_block = 1, None, (None,)
        lead_of = lambda b, r, j, u: (b,)
        q0, kv0 = M_QC0 // W_QG, M_KVC0 // W_KV
        out_shape = jax.ShapeDtypeStruct((B, sub, W_C), F32)
    else:
        B, d, sub, _ = x.shape
        tq = min(BAND_ROWS, sub)
        classes = min(d, BAND_ROWS // tq)
        lead_block = (None, classes)
        lead_of = lambda b, r, j, u: (b, r)
        q0, kv0 = 0, W_C // W_KV
        out_shape = jax.ShapeDtypeStruct((B, d, sub, W_C), F32)
    nblk = sub // tq
    in_specs = _band_specs(lead_block, lead_of, tq, nblk, lambda b, r, j, u: q0 + j, lambda b, r, j, u: kv0 + j)
    in_specs += [pl.BlockSpec((1, GQA * BLOCK, BLOCK + 2 * C_RADIUS), lambda b, r, j, u: (j, 0, 0))]
    out_spec = pl.BlockSpec(lead_block + (tq, W_QG), lambda b, r, j, u: lead_of(b, r, j, u) + (u, j))
    return pl.pallas_call(
        functools.partial(_band_kernel, nsub=tq // BLOCK, blk_axis=3, with_sink=False, halo=C_RADIUS,
                          classes=classes),
        grid=(B, d // (classes or 1), KV_C, nblk),
        in_specs=in_specs,
        out_specs=[out_spec, out_spec],
        out_shape=[out_shape, out_shape],
        compiler_params=pltpu.CompilerParams(dimension_semantics=("parallel",) * 4, vmem_limit_bytes=VMEM_LIMIT),
        name=f"mixer_c{group}",
    )(x, x, x, x, bias)


def _stage_residue_classes(o1_ref, l1_ref, o2_ref, l2_ref, stage_ref, tm):
    nk = W_C // LANES
    for a, (ref, g) in enumerate(((o1_ref, 1), (l1_ref, 1), (o2_ref, 2), (l2_ref, 2))):
        d = C_PATTERNS[g][1]
        for r in range(d):
            for k in range(nk):
                stage_ref[a * nk + k, pl.ds(r, tm // d, stride=d), :] = ref[0, r, :, k * LANES:(k + 1) * LANES]


def _merge_groups_tile(o0_ref, l0_ref, gate_ref, stage_ref, rows):
    nk = W_C // LANES
    tiles = []
    for k in range(nk):
        cols = slice(k * LANES, (k + 1) * LANES)
        o0, l0 = o0_ref[0, rows, cols], l0_ref[0, rows, cols]
        o1, l1 = stage_ref[k, rows, :], stage_ref[nk + k, rows, :]
        o2, l2 = stage_ref[2 * nk + k, rows, :], stage_ref[3 * nk + k, rows, :]
        m = jnp.maximum(jnp.maximum(l0, l1), l2)
        w0, w1, w2 = jnp.exp2(l0 - m), jnp.exp2(l1 - m), jnp.exp2(l2 - m)
        y = (w0 * o0 + w1 * o1 + w2 * o2) / (w0 + w1 + w2)
        tiles.append((y * _silu(gate_ref[0, rows, cols].astype(F32))).astype(BF16))
    return jnp.concatenate(tiles, axis=1)


def _global_kernel(q_ref, kv_ref, gate_ref, o_ref, v_s, kt_s, *, seq, tk, prep_rows):
    @pl.when(pl.program_id(2) == 0)
    def _stage_keys_values():
        def body(c, carry):
            rows = pl.ds(pl.multiple_of(c * prep_rows, prep_rows), prep_rows)
            v_s[rows, :] = jnp.concatenate([kv_ref[rows, HEAD_DIM:], jnp.ones((prep_rows, HEAD_DIM), BF16)], axis=1)
            return carry

        lax.fori_loop(0, seq // prep_rows, body, 0)
        for c in range(seq // prep_rows):
            rows = slice(c * prep_rows, (c + 1) * prep_rows)
            kt_s[:, rows] = kv_ref[rows, :HEAD_DIM].T

    q4 = _stack_heads(q_ref[...])
    rows4 = q4.shape[0]
    m = jnp.full((rows4, 1), NEG_INF, F32)
    acc = jnp.zeros((rows4, 2 * HEAD_DIM), F32)
    for c in range(seq // tk):
        rows = slice(c * tk, (c + 1) * tk)
        s = jnp.dot(q4, kt_s[:, rows], preferred_element_type=F32)
        m_new = jnp.maximum(m, jnp.max(s, axis=-1, keepdims=True))
        p = jnp.exp2(s - m_new).astype(BF16)
        acc = jnp.exp2(m - m_new) * acc + jnp.dot(p, v_s[rows, :], preferred_element_type=F32)
        m = m_new
    o = acc[:, :HEAD_DIM] / acc[:, HEAD_DIM:HEAD_DIM + 1]
    g = gate_ref[...].astype(F32)
    o_ref[...] = (_unstack_heads(o) * _silu(g)).astype(o_ref.dtype)


def _global_axial_attention(main, proj, *, tq=512, tk=512):
    B, S, _ = main.shape
    return pl.pallas_call(
        functools.partial(_global_kernel, seq=S, tk=tk, prep_rows=512),
        grid=(B, KV_B, S // tq),
        in_specs=[pl.BlockSpec((None, tq, W_QG), lambda b, h, i: (b, i, M_QB // W_QG + h)),
                  pl.BlockSpec((None, S, W_KV), lambda b, h, i: (b, 0, M_KVB // W_KV + h)),
                  pl.BlockSpec((None, tq, W_QG), lambda b, h, i: (b, i, OFF_GB // W_QG + h))],
        out_specs=pl.BlockSpec((None, tq, W_QG), lambda b, h, i: (b, i, h)),
        out_shape=jax.ShapeDtypeStruct((B, S, H_B * HEAD_DIM), BF16),
        scratch_shapes=[pltpu.VMEM((S, 2 * HEAD_DIM), BF16), pltpu.VMEM((HEAD_DIM, S), BF16)],
        compiler_params=pltpu.CompilerParams(dimension_semantics=("parallel", "parallel", "arbitrary"),
                                             vmem_limit_bytes=VMEM_LIMIT),
        name="mixer_b",
    )(main, main, proj)


def _outproj_kernel(x_ref, a_ref, b_ref, o0_ref, l0_ref, o1_ref, l1_ref, o2_ref, l2_ref, gate_ref,
                    wa_ref, wb_ref, wc_ref, o_ref, stage_ref, *, tm):
    _stage_residue_classes(o1_ref, l1_ref, o2_ref, l2_ref, stage_ref, tm)
    piece = tm // OUTPROJ_PIECES
    for h in range(OUTPROJ_PIECES):
        rows = slice(h * piece, (h + 1) * piece)
        y = jnp.dot(a_ref[0, rows, :], wa_ref[...], preferred_element_type=F32)
        y += jnp.dot(b_ref[0, rows, :], wb_ref[...], preferred_element_type=F32)
        mix_c = _merge_groups_tile(o0_ref, l0_ref, gate_ref, stage_ref, rows)
        y += jnp.dot(mix_c, wc_ref[...], preferred_element_type=F32)
        o_ref[0, rows, :] = x_ref[0, rows, :] + y


def _outproj(x, mix_a, mix_b, outs, lses, proj, w, *, tm=512):
    B, S, _ = x.shape
    ka, kb, kc = H_A * HEAD_DIM, H_B * HEAD_DIM, W_C
    d1, d2 = C_PATTERNS[1][1], C_PATTERNS[2][1]

    def tok(width, col=0):
        return pl.BlockSpec((1, tm, width), lambda b, i: (b, i, col))

    def weight(rows, blk):
        return pl.BlockSpec((rows, D_MODEL), lambda b, i: (blk, 0), pipeline_mode=pl.Buffered(1))

    res1 = pl.BlockSpec((1, d1, tm // d1, kc), lambda b, i: (b, 0, i, 0))
    res2 = pl.BlockSpec((1, d2, tm // d2, kc), lambda b, i: (b, 0, i, 0))
    return pl.pallas_call(
        functools.partial(_outproj_kernel, tm=tm),
        grid=(B, S // tm),
        in_specs=[tok(D_MODEL), tok(ka), tok(kb), tok(kc), tok(kc), res1, res1, res2, res2, tok(kc, OFF_GC // kc),
                  weight(ka, 0), weight(kb, 1), weight(kc, (ka + kb) // kc)],
        out_specs=tok(D_MODEL),
        out_shape=jax.ShapeDtypeStruct((B, S, D_MODEL), F32),
        scratch_shapes=[pltpu.VMEM((4 * (kc // LANES), tm, LANES), F32)],
        compiler_params=pltpu.CompilerParams(dimension_semantics=("parallel", "parallel"),
                                             vmem_limit_bytes=VMEM_LIMIT_OUTPROJ),
        name="outproj",
    )(x, mix_a, mix_b, outs[0], lses[0], outs[1], lses[1], outs[2], lses[2], proj, w, w, w)


def _rope_table(S):
    rows = S // GRID_W
    row = jnp.repeat(jnp.arange(rows, dtype=jnp.int32), GRID_W)
    col = jnp.arange(S, dtype=jnp.int32) % GRID_W
    n_freq = HEAD_DIM // 4
    inv_freq = ROPE_THETA ** (-jnp.arange(n_freq, dtype=F32) / n_freq)
    ang_row = row.astype(F32)[:, None] * inv_freq[None, :]
    ang_col = col.astype(F32)[:, None] * inv_freq[None, :]
    cr, sr, cc, sc = jnp.cos(ang_row), jnp.sin(ang_row), jnp.cos(ang_col), jnp.sin(ang_col)
    cos = jnp.concatenate([cr, cr, cc, cc], axis=1)
    sin_signed = jnp.concatenate([-sr, sr, -sc, sc], axis=1)
    one, zero = jnp.ones_like(cos), jnp.zeros_like(cos)
    return jnp.concatenate([cos, cos, sin_signed, sin_signed, cos, one, sin_signed, zero], axis=1)


def _trunk(x, p, rope):
    B, S, _ = x.shape
    assert rope.shape[0] >= S
    for l in range(DEPTH):
        proj = _inproj(x.reshape(B * S, D_MODEL), p["ln_g"][l], p["w_in"][l]).reshape(B, S, D_IN_PROJ)
        main, xc1, xc2 = _prepare(proj, p["wvec"][l], rope, p["seg_ones"])
        mix_a = _window_sink_attention(main, proj, p["bias_a"], p["sink_a"][l])
        mix_b = _global_axial_attention(main, proj)
        outs, lses = zip(*[_dilated_group_attention(xg, p["bias_c"][g], g) for g, xg in enumerate((main, xc1, xc2))])
        x = _outproj(x, mix_a, mix_b, outs, lses, proj, p["w_out"][l])
    return x


def _prepare_params(ln_g, w_in, q_norm_a, k_norm_a, sink_a, q_norm_b, k_norm_b, q_norm_c, k_norm_c, rel_bias, w_out):
    heads_c = [list(range(H_A + g * H_C, H_A + (g + 1) * H_C)) for g in range(N_C_GROUPS)]
    one = jnp.ones((DEPTH, HEAD_DIM), F32)

    def q_cols(w, n):
        return jnp.tile(w.astype(F32) * Q_SCALE, (1, n))

    def kv_cols(w, n):
        return jnp.tile(jnp.concatenate([w.astype(F32), one], axis=1), (1, n))

    wvec = jnp.concatenate([q_cols(q_norm_a, H_A), q_cols(q_norm_b, H_B), q_cols(q_norm_c, N_C_GROUPS * H_C),
                            kv_cols(k_norm_a, KV_A), kv_cols(k_norm_b, KV_B), kv_cols(k_norm_c, N_C_GROUPS * KV_C)],
                           axis=1)
    seg = np.arange(LANES) // HEAD_DIM
    return {
        "ln_g": ln_g.reshape(DEPTH, 1, D_MODEL),
        "w_in": _permute_columns(w_in.astype(BF16), _proj_column_order()),
        "w_out": w_out.astype(BF16),
        "wvec": wvec.reshape(DEPTH, 1, W_QKV),
        "seg_ones": jnp.asarray((seg[:, None] == seg[None, :]) / HEAD_DIM, BF16),
        "sink_a": jnp.broadcast_to(jnp.repeat(sink_a.astype(F32) * LOG2E, BLOCK, axis=1)[:, :, None],
                                   (DEPTH, H_A * BLOCK, LANES)).reshape(DEPTH, KV_A, GQA * BLOCK, LANES),
        "bias_a": _band_bias(rel_bias, list(range(H_A)), 1, WIN_A, WIN_A),
        "bias_c": [_band_bias(rel_bias, heads_c[g], d, C_RADIUS, C_RADIUS) for g, (w, d) in enumerate(C_PATTERNS)],
    }


def kernel(x_prompt, x_sample, ln_g, w_in, q_norm_a, k_norm_a, sink_a, q_norm_b, k_norm_b,
           q_norm_c, k_norm_c, rel_bias, w_out):
    p = _prepare_params(ln_g, w_in, q_norm_a, k_norm_a, sink_a, q_norm_b, k_norm_b, q_norm_c, k_norm_c,
                        rel_bias, w_out)
    rope = _rope_table(max(x_prompt.shape[1], x_sample.shape[1]))
    return _trunk(x_prompt, p, rope), _trunk(x_sample, p, rope)
```

```python
import functools
import math

import numpy as np
import jax
import jax.numpy as jnp
from jax import lax
from jax.experimental import pallas as pl
from jax.experimental.pallas import tpu as pltpu

F32 = jnp.float32
BF16 = jnp.bfloat16

D_MODEL = 2048
DEPTH = 2
HEAD_DIM = 64
N_HEADS_TOTAL = D_MODEL // HEAD_DIM
H_C = N_HEADS_TOTAL // 4
H_A = (N_HEADS_TOTAL - H_C) // 2
H_B = N_HEADS_TOTAL - H_C - H_A
KV_A = H_A // 4
KV_B = H_B // 4
KV_C = H_C // 4
GQA = 4
C_PATTERNS = ((128, 1), (512, 4), (2048, 16))
N_C_GROUPS = len(C_PATTERNS)
C_RADIUS = 64
assert all(w // (2 * d) == C_RADIUS for w, d in C_PATTERNS)
BLOCK = 128
LANES = 128
WIN_A = 128
GRID_W = 64
ROPE_THETA = 10000.0
T5_BUCKETS = 32
T5_MAX_DISTANCE = 1024
RMS_EPS = 1e-6
NEG_INF = -1e30
LOG2E = math.log2(math.e)
Q_SCALE = HEAD_DIM ** -0.5 * LOG2E

W_QG = GQA * HEAD_DIM
W_KV = 2 * HEAD_DIM
W_C = H_C * HEAD_DIM

OFF_QA = 0
OFF_QB = OFF_QA + H_A * HEAD_DIM
OFF_QC = OFF_QB + H_B * HEAD_DIM
OFF_KVA = OFF_QC + N_C_GROUPS * W_C
OFF_KVB = OFF_KVA + KV_A * W_KV
OFF_KVC = OFF_KVB + KV_B * W_KV
W_QKV = OFF_KVC + N_C_GROUPS * KV_C * W_KV
OFF_GA = W_QKV
OFF_GB = OFF_GA + H_A * HEAD_DIM
OFF_GC = OFF_GB + H_B * HEAD_DIM
D_IN_PROJ = OFF_GC + W_C

M_QA = 0
M_QB = M_QA + H_A * HEAD_DIM
M_QC0 = M_QB + H_B * HEAD_DIM
M_KVA = M_QC0 + W_C
M_KVB = M_KVA + KV_A * W_KV
M_KVC0 = M_KVB + KV_B * W_KV
W_MAIN = M_KVC0 + KV_C * W_KV
W_XC = W_C + KV_C * W_KV

MIB = 1024 * 1024
VMEM_LIMIT = 48 * MIB
VMEM_LIMIT_OUTPROJ = 54 * MIB
VMEM_LIMIT_INPROJ = 56 * MIB
OUTPROJ_PIECES = 2
BAND_ROWS = 1024


def _proj_column_order():
    heads = (H_A, KV_A, KV_A, H_A, H_B, KV_B, KV_B, H_B,
             N_C_GROUPS * H_C, N_C_GROUPS * KV_C, N_C_GROUPS * KV_C, H_C)
    o = np.concatenate([[0], np.cumsum(heads)]) * HEAD_DIM
    qa, ka, va, ga, qb, kb, vb, gb, qc, kc, vc, gc = (np.arange(o[i], o[i + 1]) for i in range(12))

    def kv_pairs(k, v, n):
        head = lambda a, h: a[h * HEAD_DIM:(h + 1) * HEAD_DIM]
        return np.concatenate([np.concatenate([head(k, h), head(v, h)]) for h in range(n)])

    order = np.concatenate([qa, qb, qc, kv_pairs(ka, va, KV_A), kv_pairs(kb, vb, KV_B),
                            kv_pairs(kc, vc, N_C_GROUPS * KV_C), ga, gb, gc])
    assert order.shape == (D_IN_PROJ,) and np.array_equal(np.sort(order), np.arange(D_IN_PROJ))
    return order


def _permute_columns(w, order):
    cuts = np.flatnonzero(np.diff(order) != 1) + 1
    starts = np.concatenate([[0], cuts])
    stops = np.concatenate([cuts, [order.size]])
    return jnp.concatenate([w[..., int(order[a]):int(order[b - 1]) + 1] for a, b in zip(starts, stops)], axis=-1)


def _t5_bucket(rel):
    half = T5_BUCKETS // 2
    max_exact = half // 2
    n = np.abs(rel)
    log_ratio = np.log(np.maximum(n, 1).astype(np.float64) / max_exact) / math.log(T5_MAX_DISTANCE / max_exact)
    large = np.minimum(max_exact + (log_ratio * (half - max_exact)).astype(np.int32), half - 1)
    return np.where(rel > 0, half, 0) + np.where(n < max_exact, n, large)


def _band_bias(rel_bias, heads, dilation, radius, halo):
    width = BLOCK + 2 * halo
    n = width + BLOCK - 1
    rel = np.arange(n) - (halo + BLOCK - 1)
    tab = rel_bias[_t5_bucket(rel * dilation)][:, heads[0]:heads[0] + len(heads)].astype(F32) * LOG2E
    tab = jnp.where((np.abs(rel) <= radius)[:, None], tab, NEG_INF)
    stream = jnp.tile(jnp.pad(tab, ((0, 1), (0, 0))), (BLOCK, 1))[:BLOCK * n]
    b = stream.reshape(BLOCK, n, len(heads))[:, BLOCK - 1:, :]
    return b.transpose(2, 0, 1).reshape(len(heads) // GQA, GQA * BLOCK, width)


def _rms(x, w):
    return x * lax.rsqrt(jnp.mean(x * x, axis=-1, keepdims=True) + RMS_EPS) * w


def _stack_heads(x):
    return jnp.concatenate([x[:, HEAD_DIM * r:HEAD_DIM * (r + 1)] for r in range(GQA)], axis=0)


def _unstack_heads(x):
    rows = x.shape[0] // GQA
    return jnp.concatenate([x[rows * r:rows * (r + 1)] for r in range(GQA)], axis=1)


def _head_rows(x, r):
    return x[r * BLOCK:(r + 1) * BLOCK]


def _silu(g):
    return g * jax.nn.sigmoid(g)


def _nt_dot(a, b):
    return lax.dot_general(a, b, (((1,), (1,)), ((), ())), preferred_element_type=F32)


def _inproj_kernel(x_ref, g_ref, w_ref, o_ref):
    h = _rms(x_ref[...], g_ref[...]).astype(BF16)
    o_ref[...] = jnp.dot(h, w_ref[...], preferred_element_type=F32).astype(o_ref.dtype)


def _inproj(x2, g, w, *, tm=512):
    T = x2.shape[0]
    return pl.pallas_call(
        _inproj_kernel,
        grid=(T // tm,),
        in_specs=[pl.BlockSpec((tm, D_MODEL), lambda i: (i, 0)),
                  pl.BlockSpec((1, D_MODEL), lambda i: (0, 0)),
                  pl.BlockSpec((D_MODEL, D_IN_PROJ), lambda i: (0, 0), pipeline_mode=pl.Buffered(1))],
        out_specs=pl.BlockSpec((tm, D_IN_PROJ), lambda i: (i, 0)),
        out_shape=jax.ShapeDtypeStruct((T, D_IN_PROJ), BF16),
        compiler_params=pltpu.CompilerParams(dimension_semantics=("parallel",),
                                             vmem_limit_bytes=VMEM_LIMIT_INPROJ),
        name="inproj",
    )(x2, g, w)


def _prep_plan():
    plan = []
    for cb in range(W_QKV // LANES):
        col = cb * LANES
        if col < OFF_QB:
            plan.append((False, False, "main", M_QA + col - OFF_QA))
        elif col < OFF_QC:
            plan.append((False, True, "main", M_QB + col - OFF_QB))
        elif col < OFF_KVA:
            g, off = divmod(col - OFF_QC, W_C)
            plan.append((False, False, "main", M_QC0 + off) if g == 0 else (False, False, f"xc{g}", off))
        elif col < OFF_KVB:
            plan.append((True, False, "main", M_KVA + col - OFF_KVA))
        elif col < OFF_KVC:
            plan.append((True, True, "main", M_KVB + col - OFF_KVB))
        else:
            g, off = divmod(col - OFF_KVC, KV_C * W_KV)
            plan.append((True, False, "main", M_KVC0 + off) if g == 0 else (True, False, f"xc{g}", W_C + off))
    return plan


def _prep_kernel(x_ref, w_ref, rope_ref, ones_ref, main_ref, xc1_ref, xc2_ref, stage_ref, *, tp, plan):
    lane = lax.broadcasted_iota(jnp.int32, (tp, LANES), 1)
    is_k = lane < HEAD_DIM
    swap_hi = (lane & 16) != 0
    seg_mean = ones_ref[...]
    n_stage = W_XC // LANES
    for cb, (is_kv, rotary, dest, col) in enumerate(plan):
        cols = slice(cb * LANES, (cb + 1) * LANES)
        x = x_ref[0, :, cols].astype(F32)
        inv = lax.rsqrt(jnp.dot((x * x).astype(BF16), seg_mean, preferred_element_type=F32) + RMS_EPS)
        if is_kv:
            inv = jnp.where(is_k, inv, 1.0)
        y = x * inv * w_ref[:, cols]
        if rotary:
            base = 2 * LANES if is_kv else 0
            cos = rope_ref[:, base:base + LANES]
            sin_signed = rope_ref[:, base + LANES:base + 2 * LANES]
            swapped = jnp.where(swap_hi, pltpu.roll(y, 16, 1), pltpu.roll(y, LANES - 16, 1))
            y = y * cos + swapped * sin_signed
        if dest == "main":
            main_ref[0, :, col:col + LANES] = y.astype(BF16)
        else:
            stage_ref[(int(dest[2]) - 1) * n_stage + col // LANES] = y
    for g, xc_ref in ((1, xc1_ref), (2, xc2_ref)):
        d = C_PATTERNS[g][1]
        for r in range(d):
            for k in range(n_stage):
                piece = stage_ref[(g - 1) * n_stage + k, pl.ds(r, tp // d, stride=d), :]
                xc_ref[0, r, :, k * LANES:(k + 1) * LANES] = piece.astype(BF16)


def _prepare(proj, wvec, rope, seg_ones, *, tp=512):
    B, S, _ = proj.shape
    d1, d2 = C_PATTERNS[1][1], C_PATTERNS[2][1]
    plan = _prep_plan()
    return pl.pallas_call(
        functools.partial(_prep_kernel, tp=tp, plan=plan),
        grid=(B, S // tp),
        in_specs=[pl.BlockSpec((1, tp, W_QKV), lambda b, i: (b, i, 0)),
                  pl.BlockSpec((1, W_QKV), lambda b, i: (0, 0)),
                  pl.BlockSpec((tp, 4 * LANES), lambda b, i: (i, 0)),
                  pl.BlockSpec((LANES, LANES), lambda b, i: (0, 0))],
        out_specs=[pl.BlockSpec((1, tp, W_MAIN), lambda b, i: (b, i, 0)),
                   pl.BlockSpec((1, d1, tp // d1, W_XC), lambda b, i: (b, 0, i, 0)),
                   pl.BlockSpec((1, d2, tp // d2, W_XC), lambda b, i: (b, 0, i, 0))],
        out_shape=[jax.ShapeDtypeStruct((B, S, W_MAIN), BF16),
                   jax.ShapeDtypeStruct((B, d1, S // d1, W_XC), BF16),
                   jax.ShapeDtypeStruct((B, d2, S // d2, W_XC), BF16)],
        scratch_shapes=[pltpu.VMEM((2 * (W_XC // LANES), tp, LANES), F32)],
        compiler_params=pltpu.CompilerParams(dimension_semantics=("parallel", "parallel"),
                                             vmem_limit_bytes=VMEM_LIMIT),
        name="prepare",
    )(proj, wvec, rope, seg_ones)


def _band_kernel(*refs, nsub, blk_axis, with_sink, halo, classes):
    if classes is None:
        _band_rows(*refs, nsub=nsub, blk_axis=blk_axis, with_sink=with_sink, halo=halo)
        return
    q_ref, kp_ref, kc_ref, kn_ref, bias_ref, o_ref, l_ref = refs
    for c in range(classes):
        _band_rows(q_ref.at[c], kp_ref.at[c], kc_ref.at[c], kn_ref.at[c], bias_ref, o_ref.at[c], l_ref.at[c],
                   nsub=nsub, blk_axis=blk_axis, with_sink=with_sink, halo=halo)


def _band_rows(*refs, nsub, blk_axis, with_sink, halo):
    if with_sink:
        q_ref, kp_ref, kc_ref, kn_ref, bias_ref, sink_ref, gate_ref, o_ref = refs
    else:
        q_ref, kp_ref, kc_ref, kn_ref, bias_ref, o_ref, l_ref = refs
    i = pl.program_id(blk_axis)
    nb = pl.num_programs(blk_axis)
    kv = jnp.concatenate([kp_ref[...], kc_ref[...], kn_ref[...]], axis=0)
    bias = bias_ref[0]
    width = BLOCK + 2 * halo
    col = lax.broadcasted_iota(jnp.int32, bias.shape, 1)
    low = lax.broadcasted_iota(jnp.int32, (BLOCK, LANES), 1) < HEAD_DIM
    for t in range(nsub):
        rows = slice(t * BLOCK, (t + 1) * BLOCK)
        q4 = _stack_heads(q_ref[rows, :])
        start = (t + 1) * BLOCK - halo
        win = kv[start:start + width]
        s = _nt_dot(q4, win[:, :HEAD_DIM]) + bias
        if t == 0:
            s = jnp.where((col < halo) & (i == 0), NEG_INF, s)
        if t == nsub - 1:
            s = jnp.where((col >= BLOCK + halo) & (i == nb - 1), NEG_INF, s)
        m = jnp.broadcast_to(jnp.max(s, axis=-1, keepdims=True), (GQA * BLOCK, LANES))
        if with_sink:
            sink = sink_ref[0]
            m = jnp.maximum(m, sink)
        e = jnp.exp2(s - jnp.concatenate([m] * (width // LANES), axis=1)).astype(BF16)
        pv = jnp.dot(e, jnp.concatenate([win, jnp.ones((width, LANES), BF16)], axis=1), preferred_element_type=F32)
        denom = pv[:, LANES:]
        if with_sink:
            denom = denom + jnp.exp2(sink - m)
        o = pv[:, :LANES] / denom
        pairs = [jnp.where(low, pltpu.roll(_head_rows(o, 2 * p), HEAD_DIM, 1), _head_rows(o, 2 * p + 1))
                 for p in range(GQA // 2)]
        o2 = jnp.concatenate(pairs, axis=1)
        if with_sink:
            g = gate_ref[rows, :].astype(F32)
            o_ref[rows, :] = (o2 * _silu(g)).astype(o_ref.dtype)
        else:
            o_ref[rows, :] = o2
            lse = m + jnp.log2(denom)
            l_ref[rows, :] = jnp.concatenate(
                [jnp.where(low, _head_rows(lse, 2 * p), _head_rows(lse, 2 * p + 1)) for p in range(GQA // 2)], axis=1)


def _band_specs(lead_block, lead_of, tq, nblk, q_blk, kv_blk):
    ratio = tq // BLOCK

    def spec(rows, width, row_fn, col_fn):
        return pl.BlockSpec(lead_block + (rows, width), lambda *g: lead_of(*g) + (row_fn(g[-1]), col_fn(*g)))

    return [spec(tq, W_QG, lambda u: u, q_blk),
            spec(BLOCK, W_KV, lambda u: jnp.maximum(u * ratio - 1, 0), kv_blk),
            spec(tq, W_KV, lambda u: u, kv_blk),
            spec(BLOCK, W_KV, lambda u: jnp.minimum((u + 1) * ratio, nblk * ratio - 1), kv_blk)]


def _window_sink_attention(main, proj, bias, sink):
    B, S, _ = main.shape
    tq = min(BAND_ROWS, S)
    nblk = S // tq
    lead_of = lambda b, h, u: (b,)
    in_specs = _band_specs((None,), lead_of, tq, nblk,
                           lambda b, h, u: M_QA // W_QG + h, lambda b, h, u: M_KVA // W_KV + h)
    in_specs += [pl.BlockSpec((1, GQA * BLOCK, BLOCK + 2 * WIN_A), lambda b, h, u: (h, 0, 0)),
                 pl.BlockSpec((1, GQA * BLOCK, LANES), lambda b, h, u: (h, 0, 0)),
                 pl.BlockSpec((None, tq, W_QG), lambda b, h, u: (b, u, OFF_GA // W_QG + h))]
    return pl.pallas_call(
        functools.partial(_band_kernel, nsub=tq // BLOCK, blk_axis=2, with_sink=True, halo=WIN_A, classes=None),
        grid=(B, KV_A, nblk),
        in_specs=in_specs,
        out_specs=pl.BlockSpec((None, tq, W_QG), lambda b, h, u: (b, u, h)),
        out_shape=jax.ShapeDtypeStruct((B, S, H_A * HEAD_DIM), BF16),
        compiler_params=pltpu.CompilerParams(dimension_semantics=("parallel",) * 3, vmem_limit_bytes=VMEM_LIMIT),
        name="mixer_a",
    )(main, main, main, main, bias, sink, proj)


def _dilated_group_attention(x, bias, group):
    if group == 0:
        B, sub, _ = x.shape
        tq = min(BAND_ROWS, sub)
        d, classes, lead_block = 1, None, (None,)
        lead_of = lambda b, r, j, u: (b,)
        q0, kv0 = M_QC0 // W_QG, M_KVC0 // W_KV
        out_shape = jax.ShapeDtypeStruct((B, sub, W_C), F32)
    else:
        B, d, sub, _ = x.shape
        tq = min(BAND_ROWS, sub)
        classes = min(d, BAND_ROWS // tq)
        lead_block = (None, classes)
        lead_of = lambda b, r, j, u: (b, r)
        q0, kv0 = 0, W_C // W_KV
        out_shape = jax.ShapeDtypeStruct((B, d, sub, W_C), F32)
    nblk = sub // tq
    in_specs = _band_specs(lead_block, lead_of, tq, nblk, lambda b, r, j, u: q0 + j, lambda b, r, j, u: kv0 + j)
    in_specs += [pl.BlockSpec((1, GQA * BLOCK, BLOCK + 2 * C_RADIUS), lambda b, r, j, u: (j, 0, 0))]
    out_spec = pl.BlockSpec(lead_block + (tq, W_QG), lambda b, r, j, u: lead_of(b, r, j, u) + (u, j))
    return pl.pallas_call(
        functools.partial(_band_kernel, nsub=tq // BLOCK, blk_axis=3, with_sink=False, halo=C_RADIUS,
                          classes=classes),
        grid=(B, d // (classes or 1), KV_C, nblk),
        in_specs=in_specs,
        out_specs=[out_spec, out_spec],
        out_shape=[out_shape, out_shape],
        compiler_params=pltpu.CompilerParams(dimension_semantics=("parallel",) * 4, vmem_limit_bytes=VMEM_LIMIT),
        name=f"mixer_c{group}",
    )(x, x, x, x, bias)


def _stage_residue_classes(o1_ref, l1_ref, o2_ref, l2_ref, stage_ref, tm):
    nk = W_C // LANES
    for a, (ref, g) in enumerate(((o1_ref, 1), (l1_ref, 1), (o2_ref, 2), (l2_ref, 2))):
        d = C_PATTERNS[g][1]
        for r in range(d):
            for k in range(nk):
                stage_ref[a * nk + k, pl.ds(r, tm // d, stride=d), :] = ref[0, r, :, k * LANES:(k + 1) * LANES]


def _merge_groups_tile(o0_ref, l0_ref, gate_ref, stage_ref, rows):
    nk = W_C // LANES
    tiles = []
    for k in range(nk):
        cols = slice(k * LANES, (k + 1) * LANES)
        o0, l0 = o0_ref[0, rows, cols], l0_ref[0, rows, cols]
        o1, l1 = stage_ref[k, rows, :], stage_ref[nk + k, rows, :]
        o2, l2 = stage_ref[2 * nk + k, rows, :], stage_ref[3 * nk + k, rows, :]
        m = jnp.maximum(jnp.maximum(l0, l1), l2)
        w0, w1, w2 = jnp.exp2(l0 - m), jnp.exp2(l1 - m), jnp.exp2(l2 - m)
        y = (w0 * o0 + w1 * o1 + w2 * o2) / (w0 + w1 + w2)
        tiles.append((y * _silu(gate_ref[0, rows, cols].astype(F32))).astype(BF16))
    return jnp.concatenate(tiles, axis=1)


def _global_kernel(q_ref, kv_ref, gate_ref, o_ref, v_s, *, seq, tk, prep_rows):
    @pl.when(pl.program_id(2) == 0)
    def _stage_values():
        def body(c, carry):
            rows = pl.ds(pl.multiple_of(c * prep_rows, prep_rows), prep_rows)
            v_s[rows, :] = jnp.concatenate([kv_ref[rows, HEAD_DIM:], jnp.ones((prep_rows, HEAD_DIM), BF16)], axis=1)
            return carry

        lax.fori_loop(0, seq // prep_rows, body, 0)

    q4 = _stack_heads(q_ref[...])
    rows4 = q4.shape[0]
    m = jnp.full((rows4, 1), NEG_INF, F32)
    acc = jnp.zeros((rows4, 2 * HEAD_DIM), F32)
    for c in range(seq // tk):
        rows = slice(c * tk, (c + 1) * tk)
        s = _nt_dot(q4, kv_ref[rows, :HEAD_DIM])
        m_new = jnp.maximum(m, jnp.max(s, axis=-1, keepdims=True))
        p = jnp.exp2(s - m_new).astype(BF16)
        acc = jnp.exp2(m - m_new) * acc + jnp.dot(p, v_s[rows, :], preferred_element_type=F32)
        m = m_new
    o = acc[:, :HEAD_DIM] / acc[:, HEAD_DIM:HEAD_DIM + 1]
    g = gate_ref[...].astype(F32)
    o_ref[...] = (_unstack_heads(o) * _silu(g)).astype(o_ref.dtype)


def _global_axial_attention(main, proj, *, tq=512, tk=512):
    B, S, _ = main.shape
    return pl.pallas_call(
        functools.partial(_global_kernel, seq=S, tk=tk, prep_rows=512),
        grid=(B, KV_B, S // tq),
        in_specs=[pl.BlockSpec((None, tq, W_QG), lambda b, h, i: (b, i, M_QB // W_QG + h)),
                  pl.BlockSpec((None, S, W_KV), lambda b, h, i: (b, 0, M_KVB // W_KV + h),
                               pipeline_mode=pl.Buffered(1)),
                  pl.BlockSpec((None, tq, W_QG), lambda b, h, i: (b, i, OFF_GB // W_QG + h))],
        out_specs=pl.BlockSpec((None, tq, W_QG), lambda b, h, i: (b, i, h)),
        out_shape=jax.ShapeDtypeStruct((B, S, H_B * HEAD_DIM), BF16),
        scratch_shapes=[pltpu.VMEM((S, 2 * HEAD_DIM), BF16)],
        compiler_params=pltpu.CompilerParams(dimension_semantics=("parallel", "parallel", "arbitrary"),
                                             vmem_limit_bytes=VMEM_LIMIT),
        name="mixer_b",
    )(main, main, proj)


def _outproj_kernel(x_ref, a_ref, b_ref, o0_ref, l0_ref, o1_ref, l1_ref, o2_ref, l2_ref, gate_ref,
                    wa_ref, wb_ref, wc_ref, o_ref, stage_ref, *, tm):
    _stage_residue_classes(o1_ref, l1_ref, o2_ref, l2_ref, stage_ref, tm)
    piece = tm // OUTPROJ_PIECES
    for h in range(OUTPROJ_PIECES):
        rows = slice(h * piece, (h + 1) * piece)
        y = jnp.dot(a_ref[0, rows, :], wa_ref[...], preferred_element_type=F32)
        y += jnp.dot(b_ref[0, rows, :], wb_ref[...], preferred_element_type=F32)
        mix_c = _merge_groups_tile(o0_ref, l0_ref, gate_ref, stage_ref, rows)
        y += jnp.dot(mix_c, wc_ref[...], preferred_element_type=F32)
        o_ref[0, rows, :] = x_ref[0, rows, :] + y


def _outproj(x, mix_a, mix_b, outs, lses, proj, w, *, tm=512):
    B, S, _ = x.shape
    ka, kb, kc = H_A * HEAD_DIM, H_B * HEAD_DIM, W_C
    d1, d2 = C_PATTERNS[1][1], C_PATTERNS[2][1]

    def tok(width, col=0):
        return pl.BlockSpec((1, tm, width), lambda b, i: (b, i, col))

    def weight(rows, blk):
        return pl.BlockSpec((rows, D_MODEL), lambda b, i: (blk, 0), pipeline_mode=pl.Buffered(1))

    res1 = pl.BlockSpec((1, d1, tm // d1, kc), lambda b, i: (b, 0, i, 0))
    res2 = pl.BlockSpec((1, d2, tm // d2, kc), lambda b, i: (b, 0, i, 0))
    return pl.pallas_call(
        functools.partial(_outproj_kernel, tm=tm),
        grid=(B, S // tm),
        in_specs=[tok(D_MODEL), tok(ka), tok(kb), tok(kc), tok(kc), res1, res1, res2, res2, tok(kc, OFF_GC // kc),
                  weight(ka, 0), weight(kb, 1), weight(kc, (ka + kb) // kc)],
        out_specs=tok(D_MODEL),
        out_shape=jax.ShapeDtypeStruct((B, S, D_MODEL), F32),
        scratch_shapes=[pltpu.VMEM((4 * (kc // LANES), tm, LANES), F32)],
        compiler_params=pltpu.CompilerParams(dimension_semantics=("parallel", "parallel"),
                                             vmem_limit_bytes=VMEM_LIMIT_OUTPROJ),
        name="outproj",
    )(x, mix_a, mix_b, outs[0], lses[0], outs[1], lses[1], outs[2], lses[2], proj, w, w, w)


def _rope_table(S):
    rows = S // GRID_W
    row = jnp.repeat(jnp.arange(rows, dtype=jnp.int32), GRID_W)
    col = jnp.arange(S, dtype=jnp.int32) % GRID_W
    n_freq = HEAD_DIM // 4
    inv_freq = ROPE_THETA ** (-jnp.arange(n_freq, dtype=F32) / n_freq)
    ang_row = row.astype(F32)[:, None] * inv_freq[None, :]
    ang_col = col.astype(F32)[:, None] * inv_freq[None, :]
    cr, sr, cc, sc = jnp.cos(ang_row), jnp.sin(ang_row), jnp.cos(ang_col), jnp.sin(ang_col)
    cos = jnp.concatenate([cr, cr, cc, cc], axis=1)
    sin_signed = jnp.concatenate([-sr, sr, -sc, sc], axis=1)
    one, zero = jnp.ones_like(cos), jnp.zeros_like(cos)
    return jnp.concatenate([cos, cos, sin_signed, sin_signed, cos, one, sin_signed, zero], axis=1)


def _trunk(x, p, rope):
    B, S, _ = x.shape
    assert rope.shape[0] >= S
    for l in range(DEPTH):
        proj = _inproj(x.reshape(B * S, D_MODEL), p["ln_g"][l], p["w_in"][l]).reshape(B, S, D_IN_PROJ)
        main, xc1, xc2 = _prepare(proj, p["wvec"][l], rope, p["seg_ones"])
        mix_a = _window_sink_attention(main, proj, p["bias_a"], p["sink_a"][l])
        mix_b = _global_axial_attention(main, proj)
        outs, lses = zip(*[_dilated_group_attention(xg, p["bias_c"][g], g) for g, xg in enumerate((main, xc1, xc2))])
        x = _outproj(x, mix_a, mix_b, outs, lses, proj, p["w_out"][l])
    return x


def _prepare_params(ln_g, w_in, q_norm_a, k_norm_a, sink_a, q_norm_b, k_norm_b, q_norm_c, k_norm_c, rel_bias, w_out):
    heads_c = [list(range(H_A + g * H_C, H_A + (g + 1) * H_C)) for g in range(N_C_GROUPS)]
    one = jnp.ones((DEPTH, HEAD_DIM), F32)

    def q_cols(w, n):
        return jnp.tile(w.astype(F32) * Q_SCALE, (1, n))

    def kv_cols(w, n):
        return jnp.tile(jnp.concatenate([w.astype(F32), one], axis=1), (1, n))

    wvec = jnp.concatenate([q_cols(q_norm_a, H_A), q_cols(q_norm_b, H_B), q_cols(q_norm_c, N_C_GROUPS * H_C),
                            kv_cols(k_norm_a, KV_A), kv_cols(k_norm_b, KV_B), kv_cols(k_norm_c, N_C_GROUPS * KV_C)],
                           axis=1)
    seg = np.arange(LANES) // HEAD_DIM
    return {
        "ln_g": ln_g.reshape(DEPTH, 1, D_MODEL),
        "w_in": _permute_columns(w_in.astype(BF16), _proj_column_order()),
        "w_out": w_out.astype(BF16),
        "wvec": wvec.reshape(DEPTH, 1, W_QKV),
        "seg_ones": jnp.asarray((seg[:, None] == seg[None, :]) / HEAD_DIM, BF16),
        "sink_a": jnp.broadcast_to(jnp.repeat(sink_a.astype(F32) * LOG2E, BLOCK, axis=1)[:, :, None],
                                   (DEPTH, H_A * BLOCK, LANES)).reshape(DEPTH, KV_A, GQA * BLOCK, LANES),
        "bias_a": _band_bias(rel_bias, list(range(H_A)), 1, WIN_A, WIN_A),
        "bias_c": [_band_bias(rel_bias, heads_c[g], d, C_RADIUS, C_RADIUS) for g, (w, d) in enumerate(C_PATTERNS)],
    }


def kernel(x_prompt, x_sample, ln_g, w_in, q_norm_a, k_norm_a, sink_a, q_norm_b, k_norm_b,
           q_norm_c, k_norm_c, rel_bias, w_out):
    p = _prepare_params(ln_g, w_in, q_norm_a, k_norm_a, sink_a, q_norm_b, k_norm_b, q_norm_c, k_norm_c,
                        rel_bias, w_out)
    rope = _rope_table(max(x_prompt.shape[1], x_sample.shape[1]))
    return _trunk(x_prompt, p, rope), _trunk(x_sample, p, rope)
```
